```python
import math
import jax
import jax.numpy as jnp
from jax import lax
import numpy as np

D_MODEL = 2048
BATCH = 2
SEQ = 16384
DEPTH = 2
DEC_BATCH = 2
DEC_SEQ = 4096
PAST_LEN = 128

N_Q_HEADS = 16
N_KV_HEADS = 4
HEAD_DIM = 64
Q_GROUP = N_Q_HEADS // N_KV_HEADS
ATTN_WIDTH = N_Q_HEADS * HEAD_DIM
KV_WIDTH = N_KV_HEADS * HEAD_DIM
WINDOW = 128
BLOCK = 128
ROPE_THETA = 500000.0
ROPE_DIM = HEAD_DIM // 4
SSM_HEADS = 16
SSM_HEAD_DIM = 64
SSM_WIDTH = SSM_HEADS * SSM_HEAD_DIM
SSM_GROUPS = 2
HEADS_PER_GROUP = SSM_HEADS // SSM_GROUPS
D_STATE = 128
BC_WIDTH = SSM_GROUPS * D_STATE
CONV_WIDTH = 5
CONV_CH = SSM_WIDTH + 2 * BC_WIDTH
CHUNK = 128
MIX_WIDTH = ATTN_WIDTH + SSM_WIDTH
PROJ_WIDTH = ATTN_WIDTH + 2 * KV_WIDTH + SSM_WIDTH + CONV_CH + 2 * SSM_HEADS
SPLIT_IDX = (ATTN_WIDTH,
             ATTN_WIDTH + KV_WIDTH,
             ATTN_WIDTH + 2 * KV_WIDTH,
             ATTN_WIDTH + 2 * KV_WIDTH + SSM_WIDTH,
             ATTN_WIDTH + 2 * KV_WIDTH + SSM_WIDTH + CONV_CH)
N_EXPERT_GROUPS = 4
EXPERTS_PER_GROUP = 8
N_EXPERTS = N_EXPERT_GROUPS * EXPERTS_PER_GROUP
TOP_K = 2
D_FF_EXPERT = 1024
MOE_BLOCK = 128
EPS = 1e-6

kernel_name = "hymba_swa_sink_bissd_hmoe_encoder"


def rmsnorm(x, w):
    xf = x.astype(jnp.float32)
    xf = xf * lax.rsqrt(jnp.mean(xf * xf, axis=-1, keepdims=True) + EPS)
    return xf.astype(x.dtype) * w


def partial_rope(t, pos):
    inv_freq = ROPE_THETA ** (-jnp.arange(0, ROPE_DIM, 2, dtype=jnp.float32) / ROPE_DIM)
    ang = pos.astype(jnp.float32)[:, None] * inv_freq[None, :]
    cos = jnp.cos(ang)[None, :, None, :]
    sin = jnp.sin(ang)[None, :, None, :]
    half = ROPE_DIM // 2
    t1 = t[..., :half].astype(jnp.float32)
    t2 = t[..., half:ROPE_DIM].astype(jnp.float32)
    rot = jnp.concatenate([t1 * cos - t2 * sin, t2 * cos + t1 * sin], axis=-1).astype(t.dtype)
    return jnp.concatenate([rot, t[..., ROPE_DIM:]], axis=-1)


def windowed_sink_attention(q, k, v, sink):
    bsz, s_len = q.shape[0], q.shape[1]
    nb = s_len // BLOCK
    qb = q.reshape(bsz, nb, BLOCK, N_KV_HEADS, Q_GROUP, HEAD_DIM)

    def band(t):
        tp = jnp.pad(t, ((0, 0), (BLOCK, BLOCK), (0, 0), (0, 0)))
        tp = tp.reshape(bsz, nb + 2, BLOCK, N_KV_HEADS, HEAD_DIM)
        return jnp.concatenate([tp[:, :-2], tp[:, 1:-1], tp[:, 2:]], axis=2)

    kb = band(k)
    vb = band(v)
    scores = jnp.einsum('bnqhgd,bnkhd->bnhgqk', qb, kb).astype(jnp.float32) * (HEAD_DIM ** -0.5)
    blk = jnp.arange(nb)[:, None]
    qpos = blk * BLOCK + jnp.arange(BLOCK)[None, :]
    kpos = (blk - 1) * BLOCK + jnp.arange(3 * BLOCK)[None, :]
    valid = ((jnp.abs(qpos[:, :, None] - kpos[:, None, :]) <= WINDOW)
             & (kpos[:, None, :] >= 0) & (kpos[:, None, :] < s_len))
    scores = jnp.where(valid[None, :, None, None], scores, -jnp.inf)
    sink_l = sink.astype(jnp.float32).reshape(N_KV_HEADS, Q_GROUP)[None, None, :, :, None, None]
    m = jnp.maximum(jnp.max(scores, axis=-1, keepdims=True), sink_l)
    p = jnp.exp(scores - m)
    denom = jnp.sum(p, axis=-1, keepdims=True) + jnp.exp(sink_l - m)
    p = (p / denom).astype(v.dtype)
    o = jnp.einsum('bnhgqk,bnkhd->bnqhgd', p, vb)
    return o.reshape(bsz, s_len, ATTN_WIDTH)


def centred_dwconv(u, w, b):
    pad = CONV_WIDTH // 2
    out = lax.conv_general_dilated(u, w[:, None, :], window_strides=(1,), padding=[(pad, pad)],
                                   dimension_numbers=('NWC', 'WIO', 'NWC'),
                                   feature_group_count=u.shape[-1])
    return out + b


def segsum_exp(a_cum):
    n = a_cum.shape[-1]
    diff = a_cum[..., :, None] - a_cum[..., None, :]
    mask = jnp.tril(jnp.ones((n, n), dtype=bool))
    return jnp.exp(jnp.where(mask, diff, -jnp.inf))


def ssd_chunked(x, dt, a_head, bm, cm):
    bsz, s_len = x.shape[0], x.shape[1]
    nc = s_len // CHUNK
    xc = (x * dt[..., None]).reshape(bsz, nc, CHUNK, SSM_GROUPS, HEADS_PER_GROUP, SSM_HEAD_DIM)
    a = (dt * a_head).reshape(bsz, nc, CHUNK, SSM_GROUPS, HEADS_PER_GROUP).transpose(0, 1, 3, 4, 2)
    a_cum = jnp.cumsum(a, axis=-1)
    bc = bm.reshape(bsz, nc, CHUNK, SSM_GROUPS, D_STATE)
    cc = cm.reshape(bsz, nc, CHUNK, SSM_GROUPS, D_STATE)
    cb = jnp.einsum('bclgn,bcsgn->bcgls', cc, bc)
    y_diag = jnp.einsum('bcgls,bcgjls,bcsgjp->bclgjp', cb, segsum_exp(a_cum), xc)
    decay = jnp.exp(a_cum[..., -1:] - a_cum)
    states = jnp.einsum('bclgn,bcgjl,bclgjp->bcgjpn', bc, decay, xc)
    chunk_decay = jnp.exp(a_cum[..., -1])

    def step(h, inp):
        st, dec = inp
        return h * dec[..., None, None] + st, h

    h0 = jnp.zeros((bsz, SSM_GROUPS, HEADS_PER_GROUP, SSM_HEAD_DIM, D_STATE), x.dtype)
    _, h_prev = lax.scan(step, h0, (jnp.swapaxes(states, 0, 1), jnp.swapaxes(chunk_decay, 0, 1)))
    h_prev = jnp.swapaxes(h_prev, 0, 1)
    y_off = jnp.einsum('bclgn,bcgjpn,bcgjl->bclgjp', cc, h_prev, jnp.exp(a_cum))
    return (y_diag + y_off).reshape(bsz, s_len, SSM_HEADS, SSM_HEAD_DIM)


def mixer(h, w_in, conv_w, conv_b, attn_sink, attn_out_norm_w, ssm_a_log, ssm_dt_bias, ssm_d,
          ssm_norm_w, w_out):
    bsz, s_len, _ = h.shape
    pos = jnp.arange(s_len)
    proj = h @ w_in
    q, k, v, z, xbc, dt_raw = jnp.split(proj, SPLIT_IDX, axis=-1)
    q = partial_rope(q.reshape(bsz, s_len, N_Q_HEADS, HEAD_DIM), pos)
    k = partial_rope(k.reshape(bsz, s_len, N_KV_HEADS, HEAD_DIM), pos)
    v = v.reshape(bsz, s_len, N_KV_HEADS, HEAD_DIM)
    attn = rmsnorm(windowed_sink_attention(q, k, v, attn_sink), attn_out_norm_w)
    xbc = jax.nn.silu(centred_dwconv(xbc, conv_w, conv_b))
    xs, bm, cm = jnp.split(xbc, (SSM_WIDTH, SSM_WIDTH + BC_WIDTH), axis=-1)
    xs = xs.reshape(bsz, s_len, SSM_HEADS, SSM_HEAD_DIM).astype(jnp.float32)
    bm = bm.reshape(bsz, s_len, SSM_GROUPS, D_STATE).astype(jnp.float32)
    cm = cm.reshape(bsz, s_len, SSM_GROUPS, D_STATE).astype(jnp.float32)
    dt_all = jax.nn.softplus(dt_raw.astype(jnp.float32).reshape(bsz, s_len, 2, SSM_HEADS)
                             + ssm_dt_bias.astype(jnp.float32))
    a_heads = -jnp.exp(ssm_a_log.astype(jnp.float32))
    y_fwd = ssd_chunked(xs, dt_all[:, :, 0], a_heads[0], bm, cm)
    y_bwd = jnp.flip(ssd_chunked(jnp.flip(xs, 1), jnp.flip(dt_all[:, :, 1], 1), a_heads[1],
                                 jnp.flip(bm, 1), jnp.flip(cm, 1)), 1)
    y = y_fwd + y_bwd + xs * ssm_d.astype(jnp.float32)[:, None]
    y = y.reshape(bsz, s_len, SSM_WIDTH).astype(h.dtype)
    ssd_out = rmsnorm(y * jax.nn.silu(z), ssm_norm_w)
    return jnp.concatenate([attn, ssd_out], axis=-1) @ w_out


def hierarchical_moe(h, w_rg, b_rg, w_re, b_re, w_gate, w_up, w_down):
    bsz, s_len, d = h.shape
    xt = h.reshape(-1, d)
    n_tok = xt.shape[0]
    g_prob = jax.nn.softmax((xt @ w_rg + b_rg).astype(jnp.float32), axis=-1)
    g_sel = jnp.argmax(g_prob, axis=-1)
    g_w = jnp.take_along_axis(g_prob, g_sel[:, None], axis=1)[:, 0]
    e_logits = (xt @ w_re + b_re).astype(jnp.float32).reshape(n_tok, N_EXPERT_GROUPS, EXPERTS_PER_GROUP)
    e_in = jnp.take_along_axis(e_logits, g_sel[:, None, None], axis=1)[:, 0]
    top_w, top_i = lax.top_k(jax.nn.softmax(e_in, axis=-1), TOP_K)
    gate = g_w[:, None] * top_w / jnp.sum(top_w, axis=-1, keepdims=True)
    expert = g_sel[:, None] * EXPERTS_PER_GROUP + top_i
    n_assign = n_tok * TOP_K
    e_flat = expert.reshape(-1)
    tok_flat = jnp.repeat(jnp.arange(n_tok, dtype=jnp.int32), TOP_K)
    w_flat = gate.reshape(-1)
    order = jnp.argsort(e_flat)
    e_sorted = e_flat[order]
    counts = jnp.bincount(e_flat, length=N_EXPERTS)
    padded = (counts + MOE_BLOCK - 1) // MOE_BLOCK * MOE_BLOCK
    start = jnp.cumsum(counts) - counts
    pend = jnp.cumsum(padded)
    pstart = pend - padded
    dest = pstart[e_sorted] + jnp.arange(n_assign) - start[e_sorted]
    n_blocks = -(-n_assign // MOE_BLOCK) + N_EXPERTS
    n_slots = n_blocks * MOE_BLOCK
    slot_tok = jnp.full((n_slots,), n_tok, dtype=jnp.int32).at[dest].set(tok_flat[order])
    slot_w = jnp.zeros((n_slots,), jnp.float32).at[dest].set(w_flat[order])
    block_expert = jnp.minimum(
        jnp.searchsorted(pend, jnp.arange(n_blocks) * MOE_BLOCK, side='right'), N_EXPERTS - 1)
    x_pad = jnp.concatenate([xt, jnp.zeros((1, d), xt.dtype)], axis=0)
    xs = x_pad[slot_tok].reshape(n_blocks, MOE_BLOCK, d)

    def expert_block(args):
        xb, e = args
        hb = jax.nn.silu(xb @ w_gate[e]) * (xb @ w_up[e])
        return hb @ w_down[e]

    ys = lax.map(expert_block, (xs, block_expert)).reshape(n_slots, d)
    out = jax.ops.segment_sum(ys * slot_w[:, None].astype(ys.dtype), slot_tok, num_segments=n_tok + 1)[:n_tok]
    return out.reshape(bsz, s_len, d)


def trunk(x, attn_norm_w, w_in, conv_w, conv_b, attn_sink, attn_out_norm_w, ssm_a_log, ssm_dt_bias,
          ssm_d, ssm_norm_w, w_out, ffn_norm_w, w_router_group, b_router_group, w_router_expert,
          b_router_expert, w_gate, w_up, w_down, final_norm_w):
    for l in range(DEPTH):
        h = rmsnorm(x, attn_norm_w[l])
        x = x + mixer(h, w_in[l], conv_w[l], conv_b[l], attn_sink[l], attn_out_norm_w[l], ssm_a_log[l],
                      ssm_dt_bias[l], ssm_d[l], ssm_norm_w[l], w_out[l])
        h = rmsnorm(x, ffn_norm_w[l])
        x = x + hierarchical_moe(h, w_router_group[l], b_router_group[l], w_router_expert[l],
                                 b_router_expert[l], w_gate[l], w_up[l], w_down[l])
    return rmsnorm(x, final_norm_w)


def setup_inputs(seed: int = 0) -> dict:
    key = jax.random.key(seed)
    ks = jax.random.split(key, 24)
    f32 = jnp.float32

    def nrm(k, shape, scale):
        return jax.random.normal(k, shape, f32) * scale

    def gain(k, shape):
        return 1.0 + 0.01 * jax.random.normal(k, shape, f32)

    dt0 = jnp.exp(jax.random.uniform(ks[9], (DEPTH, 2, SSM_HEADS), f32,
                                     minval=math.log(1e-3), maxval=math.log(1e-1)))
    return {
        'x_prompt': jax.random.normal(ks[0], (BATCH, SEQ, D_MODEL), f32),
        'x_sample': jax.random.normal(ks[1], (DEC_BATCH, DEC_SEQ, D_MODEL), f32),
        'attn_norm_w': gain(ks[2], (DEPTH, D_MODEL)),
        'w_in': nrm(ks[3], (DEPTH, D_MODEL, PROJ_WIDTH), D_MODEL ** -0.5),
        'conv_w': nrm(ks[4], (DEPTH, CONV_WIDTH, CONV_CH), CONV_WIDTH ** -0.5),
        'conv_b': nrm(ks[5], (DEPTH, CONV_CH), 0.01),
        'attn_sink': nrm(ks[6], (DEPTH, N_Q_HEADS), 0.5),
        'attn_out_norm_w': gain(ks[7], (DEPTH, ATTN_WIDTH)),
        'ssm_a_log': jnp.log(jax.random.uniform(ks[8], (DEPTH, 2, SSM_HEADS), f32, minval=1.0, maxval=16.0)),
        'ssm_dt_bias': dt0 + jnp.log(-jnp.expm1(-dt0)),
        'ssm_d': gain(ks[10], (DEPTH, SSM_HEADS)),
        'ssm_norm_w': gain(ks[11], (DEPTH, SSM_WIDTH)),
        'w_out': nrm(ks[12], (DEPTH, MIX_WIDTH, D_MODEL), MIX_WIDTH ** -0.5),
        'ffn_norm_w': gain(ks[13], (DEPTH, D_MODEL)),
        'w_router_group': nrm(ks[14], (DEPTH, D_MODEL, N_EXPERT_GROUPS), D_MODEL ** -0.5),
        'b_router_group': nrm(ks[15], (DEPTH, N_EXPERT_GROUPS), 0.01),
        'w_router_expert': nrm(ks[16], (DEPTH, D_MODEL, N_EXPERTS), D_MODEL ** -0.5),
        'b_router_expert': nrm(ks[17], (DEPTH, N_EXPERTS), 0.01),
        'w_gate': nrm(ks[18], (DEPTH, N_EXPERTS, D_MODEL, D_FF_EXPERT), D_MODEL ** -0.5),
        'w_up': nrm(ks[19], (DEPTH, N_EXPERTS, D_MODEL, D_FF_EXPERT), D_MODEL ** -0.5),
        'w_down': nrm(ks[20], (DEPTH, N_EXPERTS, D_FF_EXPERT, D_MODEL), D_FF_EXPERT ** -0.5),
        'final_norm_w': gain(ks[21], (D_MODEL,)),
    }


def reference(x_prompt, x_sample, attn_norm_w, w_in, conv_w, conv_b, attn_sink, attn_out_norm_w,
              ssm_a_log, ssm_dt_bias, ssm_d, ssm_norm_w, w_out, ffn_norm_w, w_router_group,
              b_router_group, w_router_expert, b_router_expert, w_gate, w_up, w_down, final_norm_w):
    y_prompt = trunk(x_prompt, attn_norm_w, w_in, conv_w, conv_b, attn_sink, attn_out_norm_w, ssm_a_log,
                     ssm_dt_bias, ssm_d, ssm_norm_w, w_out, ffn_norm_w, w_router_group, b_router_group,
                     w_router_expert, b_router_expert, w_gate, w_up, w_down, final_norm_w)
    y_sample = trunk(x_sample, attn_norm_w, w_in, conv_w, conv_b, attn_sink, attn_out_norm_w, ssm_a_log,
                     ssm_dt_bias, ssm_d, ssm_norm_w, w_out, ffn_norm_w, w_router_group, b_router_group,
                     w_router_expert, b_router_expert, w_gate, w_up, w_down, final_norm_w)
    return (y_prompt, y_sample)
```

```python
import functools

import jax
import jax.numpy as jnp
from jax import lax
from jax.experimental import pallas as pl
from jax.experimental.pallas import tpu as pltpu

D_MODEL = 2048
DEPTH = 2
N_Q_HEADS = 16
N_KV_HEADS = 4
HEAD_DIM = 64
Q_GROUP = N_Q_HEADS // N_KV_HEADS
ATTN_WIDTH = N_Q_HEADS * HEAD_DIM
KV_WIDTH = N_KV_HEADS * HEAD_DIM
WINDOW = 128
BLOCK = 128
ROPE_THETA = 500000.0
ROPE_DIM = HEAD_DIM // 4
ROPE_HALF = ROPE_DIM // 2
SSM_HEADS = 16
SSM_HEAD_DIM = 64
SSM_WIDTH = SSM_HEADS * SSM_HEAD_DIM
SSM_GROUPS = 2
HEADS_PER_GROUP = SSM_HEADS // SSM_GROUPS
D_STATE = 128
BC_WIDTH = SSM_GROUPS * D_STATE
CONV_WIDTH = 5
CONV_PAD = CONV_WIDTH // 2
CONV_CH = SSM_WIDTH + 2 * BC_WIDTH
CHUNK = 128
N_EXPERT_GROUPS = 4
EXPERTS_PER_GROUP = 8
N_EXPERTS = N_EXPERT_GROUPS * EXPERTS_PER_GROUP
TOP_K = 2
D_FF_EXPERT = 1024
EPS = 1e-6

LANES = 128
BF16_SUBLANES = 16
MIB = 1024 * 1024

COL_Q = 0
COL_K = ATTN_WIDTH
COL_V = COL_K + KV_WIDTH
COL_XBC = COL_V + KV_WIDTH
COL_Z = COL_XBC + CONV_CH
PROJ_MAIN = COL_Z + SSM_WIDTH
PROJ_TM = 512
PROJ_TN = 1024
ATTN_TQ = 256
OUT_TM = 256
MOE_TM = 256
COMB_TM = 256
ROUTE_LANES = LANES
NEG = -1e30
F32 = jnp.float32
BF16 = jnp.bfloat16


def _cparams(semantics, vmem_mib):
    return pltpu.CompilerParams(dimension_semantics=semantics, vmem_limit_bytes=vmem_mib * MIB)


def _rms(x, w):
    ms = jnp.mean(x * x, axis=-1, keepdims=True)
    return (x * lax.rsqrt(ms + EPS)) * w


def _silu(x):
    return x / (1.0 + jnp.exp(-x))


def _split2(v):
    hi = v.astype(BF16)
    lo = (v - hi.astype(F32)).astype(BF16)
    return hi, lo


def _split3(v):
    hi = v.astype(BF16)
    r = v - hi.astype(F32)
    mid = r.astype(BF16)
    lo = (r - mid.astype(F32)).astype(BF16)
    return hi, mid, lo


def _proj_kernel(x_ref, nw_ref, w_ref, wdt_ref, c_ref, sa_ref, sb_ref, out_ref, dt_ref, h_ref):
    j = pl.program_id(1)

    @pl.when(j == 0)
    def _():
        h = _rms(x_ref[...], nw_ref[...]).astype(BF16)
        h_ref[...] = h
        dt_ref[...] = jnp.dot(h, wdt_ref[...], preferred_element_type=F32)

    acc = jnp.dot(h_ref[...], w_ref[...], preferred_element_type=F32)

    def rope(a):
        return (a * c_ref[...] + pltpu.roll(a, ROPE_HALF, 1) * sa_ref[...]
                + pltpu.roll(a, LANES - ROPE_HALF, 1) * sb_ref[...])

    def store(n_rope_tiles):
        for cb in range(PROJ_TN // LANES):
            a = acc[:, cb * LANES:(cb + 1) * LANES]
            if cb < n_rope_tiles:
                a = rope(a)
            out_ref[:, cb * LANES:(cb + 1) * LANES] = a.astype(BF16)

    @pl.when(j == COL_Q // PROJ_TN)
    def _():
        store(ATTN_WIDTH // LANES)

    @pl.when(j == COL_K // PROJ_TN)
    def _():
        store(KV_WIDTH // LANES)

    @pl.when(j > COL_K // PROJ_TN)
    def _():
        store(0)


def _proj(x, nw, w_main, w_dt, rope_c, rope_sa, rope_sb):
    t = x.shape[0]
    assert t % PROJ_TM == 0 and PROJ_MAIN % PROJ_TN == 0
    row = lambda i, j: (i, 0)
    const = lambda i, j: (0, 0)
    return pl.pallas_call(
        _proj_kernel,
        out_shape=(jax.ShapeDtypeStruct((t, PROJ_MAIN), BF16), jax.ShapeDtypeStruct((t, LANES), F32)),
        grid=(t // PROJ_TM, PROJ_MAIN // PROJ_TN),
        in_specs=[
            pl.BlockSpec((PROJ_TM, D_MODEL), row),
            pl.BlockSpec((1, D_MODEL), const),
            pl.BlockSpec((D_MODEL, PROJ_TN), lambda i, j: (0, j)),
            pl.BlockSpec((D_MODEL, LANES), const),
            pl.BlockSpec((PROJ_TM, LANES), row),
            pl.BlockSpec((PROJ_TM, LANES), row),
            pl.BlockSpec((PROJ_TM, LANES), row),
        ],
        out_specs=(pl.BlockSpec((PROJ_TM, PROJ_TN), lambda i, j: (i, j)),
                   pl.BlockSpec((PROJ_TM, LANES), row)),
        scratch_shapes=[pltpu.VMEM((PROJ_TM, D_MODEL), BF16)],
        compiler_params=_cparams(("parallel", "arbitrary"), 40),
        name="proj",
    )(x, nw, w_main, w_dt, rope_c, rope_sa, rope_sb)


def _attn_kernel(tf_ref, tl_ref, sink_ref, q_ref, kc_ref, vc_ref, kp_ref, vp_ref, kn_ref, vn_ref,
                 nw_ref, o_ref):
    i = pl.program_id(0)
    kw = jnp.concatenate([kp_ref[...], kc_ref[...], kn_ref[...]], axis=0)
    vw = jnp.concatenate([vp_ref[...], vc_ref[...], vn_ref[...]], axis=0)
    qi = lax.broadcasted_iota(jnp.int32, (BLOCK, 3 * BLOCK), 0)
    kk = lax.broadcasted_iota(jnp.int32, (BLOCK, 3 * BLOCK), 1)
    band = (kk >= qi) & (kk <= qi + 2 * WINDOW)
    lo_valid = tf_ref[i] * BLOCK
    hi_valid = 3 * BLOCK - tl_ref[i] * BLOCK
    nsb = ATTN_TQ // BLOCK
    scale = HEAD_DIM ** -0.5
    for sb in range(nsb):
        mask = band
        if sb == 0:
            mask = mask & (kk >= lo_valid)
        if sb == nsb - 1:
            mask = mask & (kk < hi_valid)
        ks = kw[sb * BLOCK:(sb + 3) * BLOCK]
        vs = vw[sb * BLOCK:(sb + 3) * BLOCK]
        q_sb = q_ref[sb * BLOCK:(sb + 1) * BLOCK, :]
        outs = []
        for h in range(N_Q_HEADS):
            g = h // Q_GROUP
            qh = q_sb[:, h * HEAD_DIM:(h + 1) * HEAD_DIM]
            kg = ks[:, g * HEAD_DIM:(g + 1) * HEAD_DIM]
            s = lax.dot_general(qh, kg, (((1,), (1,)), ((), ())), preferred_element_type=F32) * scale
            s = jnp.where(mask, s, NEG)
            sink = sink_ref[h]
            m = jnp.maximum(jnp.max(s, axis=-1, keepdims=True), sink)
            p = jnp.exp(s - m)
            denom = jnp.sum(p, axis=-1, keepdims=True) + jnp.exp(sink - m)
            o = jnp.dot(p.astype(BF16), vs[:, g * HEAD_DIM:(g + 1) * HEAD_DIM], preferred_element_type=F32)
            outs.append(o / denom)
        o = jnp.concatenate(outs, axis=1)
        o_ref[sb * BLOCK:(sb + 1) * BLOCK, :] = _rms(o, nw_ref[...]).astype(BF16)


def _attention(proj, tile_first, tile_last, sink, nw):
    t = proj.shape[0]
    assert t % ATTN_TQ == 0
    nb = t // BLOCK
    r = ATTN_TQ // BLOCK
    kcol, vcol = COL_K // KV_WIDTH, COL_V // KV_WIDTH
    prev = lambda i, *_: jnp.maximum(i * r - 1, 0)
    nxt = lambda i, *_: jnp.minimum(i * r + r, nb - 1)
    grid_spec = pltpu.PrefetchScalarGridSpec(
        num_scalar_prefetch=3,
        grid=(t // ATTN_TQ,),
        in_specs=[
            pl.BlockSpec((ATTN_TQ, ATTN_WIDTH), lambda i, *_: (i, COL_Q // ATTN_WIDTH)),
            pl.BlockSpec((ATTN_TQ, KV_WIDTH), lambda i, *_: (i, kcol)),
            pl.BlockSpec((ATTN_TQ, KV_WIDTH), lambda i, *_: (i, vcol)),
            pl.BlockSpec((BLOCK, KV_WIDTH), lambda i, *_: (prev(i), kcol)),
            pl.BlockSpec((BLOCK, KV_WIDTH), lambda i, *_: (prev(i), vcol)),
            pl.BlockSpec((BLOCK, KV_WIDTH), lambda i, *_: (nxt(i), kcol)),
            pl.BlockSpec((BLOCK, KV_WIDTH), lambda i, *_: (nxt(i), vcol)),
            pl.BlockSpec((1, ATTN_WIDTH), lambda i, *_: (0, 0)),
        ],
        out_specs=pl.BlockSpec((ATTN_TQ, ATTN_WIDTH), lambda i, *_: (i, 0)),
    )
    return pl.pallas_call(
        _attn_kernel,
        out_shape=jax.ShapeDtypeStruct((t, ATTN_WIDTH), BF16),
        grid_spec=grid_spec,
        compiler_params=_cparams(("parallel",), 32),
        name="attn",
    )(tile_first, tile_last, sink, proj, proj, proj, proj, proj, proj, proj, nw)


def _ssd_kernel(first_ref, last_ref, xc_ref, xp_ref, xn_ref, dt_ref, *rest, backward):
    if backward:
        (z_ref, yf_ref, cw_ref, cb_ref, dtb_ref, alog_ref, e_ref, nw_ref, out_ref, xw_ref, h_ref) = rest
    else:
        (cw_ref, cb_ref, dtb_ref, alog_ref, e_ref, dexp_ref, out_ref, xw_ref, h_ref) = rest
    i = pl.program_id(0)
    c = (pl.num_programs(0) - 1 - i) if backward else i
    is_first = first_ref[c] == 1
    is_last = last_ref[c] == 1

    @pl.when(is_last if backward else is_first)
    def _():
        h_ref[...] = jnp.zeros_like(h_ref)

    halo = BF16_SUBLANES
    xp = xp_ref[...].astype(F32)
    xn = xn_ref[...].astype(F32)
    xw_ref[0:halo, :] = jnp.where(is_first, jnp.zeros_like(xp), xp)
    xw_ref[halo:halo + CHUNK, :] = xc_ref[...].astype(F32)
    xw_ref[halo + CHUNK:2 * halo + CHUNK, :] = jnp.where(is_last, jnp.zeros_like(xn), xn)
    acc = cb_ref[...] + xw_ref[pl.ds(halo - CONV_PAD, CHUNK), :] * cw_ref[0:1, :]
    for k in range(1, CONV_WIDTH):
        acc = acc + xw_ref[pl.ds(halo - CONV_PAD + k, CHUNK), :] * cw_ref[k:k + 1, :]
    u = _silu(acc)
    xs = u[:, :SSM_WIDTH]
    bm = u[:, SSM_WIDTH:SSM_WIDTH + BC_WIDTH]
    cm = u[:, SSM_WIDTH + BC_WIDTH:]

    d0 = SSM_HEADS if backward else 0
    draw = dt_ref[...] + dtb_ref[...]
    dt = jnp.maximum(draw, 0.0) + jnp.log1p(jnp.exp(-jnp.abs(draw)))
    a = dt * (-jnp.exp(alog_ref[...]))
    li = lax.broadcasted_iota(jnp.int32, (CHUNK, CHUNK), 0)
    ti = lax.broadcasted_iota(jnp.int32, (CHUNK, CHUNK), 1)
    causal = (li <= ti) if backward else (li >= ti)
    tri = jnp.where(causal, 1.0, 0.0).astype(BF16)
    a_hi, a_mid, a_lo = _split3(a)
    acum = (jnp.dot(tri, a_hi, preferred_element_type=F32) + jnp.dot(tri, a_mid, preferred_element_type=F32)
            + jnp.dot(tri, a_lo, preferred_element_type=F32))
    end = 0 if backward else CHUNK - 1
    tot = acum[end:end + 1, :]
    exp_a = jnp.exp(acum)
    wdec = dt * jnp.exp(tot - acum)
    stack = jnp.concatenate(_split2(dt) + _split2(exp_a) + _split2(wdec), axis=0)
    ex = jnp.dot(stack, e_ref[...], preferred_element_type=F32)
    dt_e = ex[0:CHUNK] + ex[CHUNK:2 * CHUNK]
    ea_e = ex[2 * CHUNK:3 * CHUNK] + ex[3 * CHUNK:4 * CHUNK]
    wd_e = ex[4 * CHUNK:5 * CHUNK] + ex[5 * CHUNK:6 * CHUNK]
    xdt = xs * dt_e
    xd = (xs * wd_e).astype(BF16)
    acum_t = acum.T
    hprev = h_ref[...]
    hb = hprev.astype(BF16)
    lane = lax.broadcasted_iota(jnp.int32, (CHUNK, LANES), 1)
    gw = HEADS_PER_GROUP * SSM_HEAD_DIM
    ys = []
    sts = []
    for g in range(SSM_GROUPS):
        bg = bm[:, g * D_STATE:(g + 1) * D_STATE]
        cg = cm[:, g * D_STATE:(g + 1) * D_STATE].astype(BF16)
        cb = lax.dot_general(cg, bg.astype(BF16), (((1,), (1,)), ((), ())), preferred_element_type=F32)
        yoff = jnp.dot(cg, hb[:, g * gw:(g + 1) * gw], preferred_element_type=F32)
        sts.append(jnp.dot(bg.T.astype(BF16), xd[:, g * gw:(g + 1) * gw], preferred_element_type=F32))
        for pr in range(HEADS_PER_GROUP // 2):
            h0 = g * HEADS_PER_GROUP + 2 * pr
            ms = []
            for hh in (h0, h0 + 1):
                col = acum[:, d0 + hh:d0 + hh + 1]
                row = acum_t[d0 + hh:d0 + hh + 1, :]
                decay = jnp.exp(jnp.where(causal, col - row, NEG))
                ms.append((cb * decay).astype(BF16))
            lhs = jnp.concatenate(ms, axis=1)
            c0 = (h0 // 2) * LANES
            slab = xdt[:, c0:c0 + LANES]
            rhs = jnp.concatenate([jnp.where(lane < SSM_HEAD_DIM, slab, 0.0),
                                   jnp.where(lane >= SSM_HEAD_DIM, slab, 0.0)], axis=0).astype(BF16)
            yd = jnp.dot(lhs, rhs, preferred_element_type=F32)
            ys.append(yd + yoff[:, pr * LANES:(pr + 1) * LANES] * ea_e[:, c0:c0 + LANES])
    y = jnp.concatenate(ys, axis=1)
    h_ref[...] = hprev * ea_e[end:end + 1, :] + jnp.concatenate(sts, axis=1)

    if backward:
        yt = yf_ref[...] + y
        gz = yt * _silu(z_ref[...].astype(F32))
        out_ref[...] = _rms(gz, nw_ref[...]).astype(BF16)
    else:
        out_ref[...] = y + xs * dexp_ref[...]


def _ssd(proj, dt_raw, chunk_first, chunk_last, conv_w, conv_b, dt_bias, a_log, expand, extra, *, backward,
         y_fwd=None):
    t = proj.shape[0]
    nc = t // CHUNK
    hb = CHUNK // BF16_SUBLANES
    n_halo = t // BF16_SUBLANES
    xcol = COL_XBC // CONV_CH
    ch = (lambda i: nc - 1 - i) if backward else (lambda i: i)
    const = lambda i, *_: (0, 0)
    in_specs = [
        pl.BlockSpec((CHUNK, CONV_CH), lambda i, *_: (ch(i), xcol)),
        pl.BlockSpec((BF16_SUBLANES, CONV_CH), lambda i, *_: (jnp.maximum(ch(i) * hb - 1, 0), xcol)),
        pl.BlockSpec((BF16_SUBLANES, CONV_CH), lambda i, *_: (jnp.minimum(ch(i) * hb + hb, n_halo - 1), xcol)),
        pl.BlockSpec((CHUNK, LANES), lambda i, *_: (ch(i), 0)),
    ]
    args = [proj, proj, proj, dt_raw]
    if backward:
        in_specs += [pl.BlockSpec((CHUNK, SSM_WIDTH), lambda i, *_: (ch(i), COL_Z // SSM_WIDTH)),
                     pl.BlockSpec((CHUNK, SSM_WIDTH), lambda i, *_: (ch(i), 0))]
        args += [proj, y_fwd]
    in_specs += [
        pl.BlockSpec((8, CONV_CH), const),
        pl.BlockSpec((1, CONV_CH), const),
        pl.BlockSpec((1, LANES), const),
        pl.BlockSpec((1, LANES), const),
        pl.BlockSpec((LANES, SSM_WIDTH), const),
        pl.BlockSpec((1, SSM_WIDTH), const),
    ]
    args += [conv_w, conv_b, dt_bias, a_log, expand, extra]
    grid_spec = pltpu.PrefetchScalarGridSpec(
        num_scalar_prefetch=2,
        grid=(nc,),
        in_specs=in_specs,
        out_specs=pl.BlockSpec((CHUNK, SSM_WIDTH), lambda i, *_: (ch(i), 0)),
        scratch_shapes=[pltpu.VMEM((CHUNK + 2 * BF16_SUBLANES, CONV_CH), F32),
                        pltpu.VMEM((D_STATE, SSM_WIDTH), F32)],
    )
    return pl.pallas_call(
        functools.partial(_ssd_kernel, backward=backward),
        out_shape=jax.ShapeDtypeStruct((t, SSM_WIDTH), BF16 if backward else F32),
        grid_spec=grid_spec,
        compiler_params=_cparams(("arbitrary",), 32),
        name="ssd_bwd" if backward else "ssd_fwd",
    )(chunk_first, chunk_last, *args)


def _outproj_kernel(a_ref, s_ref, x_ref, wa_ref, ws_ref, nw_ref, wr_ref, br_ref, x1_ref, h2_ref, eid_ref,
                    gate_ref):
    y = (jnp.dot(a_ref[...], wa_ref[...], preferred_element_type=F32)
         + jnp.dot(s_ref[...], ws_ref[...], preferred_element_type=F32))
    x1 = x_ref[...] + y
    x1_ref[...] = x1
    h2 = _rms(x1, nw_ref[...])
    h2_ref[...] = h2
    logits = jnp.dot(h2, wr_ref[...], preferred_element_type=F32, precision=lax.Precision.HIGHEST) + br_ref[...]
    lane = lax.broadcasted_iota(jnp.int32, logits.shape, 1)
    big = jnp.int32(ROUTE_LANES)
    gl = jnp.where(lane < N_EXPERT_GROUPS, logits, NEG)
    gmax = jnp.max(gl, axis=-1, keepdims=True)
    gsel = jnp.min(jnp.where(gl == gmax, lane, big), axis=-1, keepdims=True)
    gsum = jnp.sum(jnp.exp(gl - gmax), axis=-1, keepdims=True)
    g_w = 1.0 / gsum
    e_lo = N_EXPERT_GROUPS + gsel * EXPERTS_PER_GROUP
    el = jnp.where((lane >= e_lo) & (lane < e_lo + EXPERTS_PER_GROUP), logits, NEG)
    m1 = jnp.max(el, axis=-1, keepdims=True)
    i1 = jnp.min(jnp.where(el == m1, lane, big), axis=-1, keepdims=True)
    el2 = jnp.where(lane == i1, NEG, el)
    m2 = jnp.max(el2, axis=-1, keepdims=True)
    i2 = jnp.min(jnp.where(el2 == m2, lane, big), axis=-1, keepdims=True)
    r = jnp.exp(m2 - m1)
    w1 = g_w / (1.0 + r)
    w2 = w1 * r
    eid_ref[...] = jnp.where(lane == 0, i1 - N_EXPERT_GROUPS, jnp.where(lane == 1, i2 - N_EXPERT_GROUPS, 0))
    gate_ref[...] = jnp.where(lane == 0, w1, jnp.where(lane == 1, w2, 0.0))


def _outproj(attn, ssd, x, wa, ws, nw, wr, br):
    t = x.shape[0]
    assert t % OUT_TM == 0
    row = lambda i: (i, 0)
    const = lambda i: (0, 0)
    return pl.pallas_call(
        _outproj_kernel,
        out_shape=(jax.ShapeDtypeStruct((t, D_MODEL), F32), jax.ShapeDtypeStruct((t, D_MODEL), F32),
                   jax.ShapeDtypeStruct((t, ROUTE_LANES), jnp.int32), jax.ShapeDtypeStruct((t, ROUTE_LANES), F32)),
        grid=(t // OUT_TM,),
        in_specs=[
            pl.BlockSpec((OUT_TM, ATTN_WIDTH), row),
            pl.BlockSpec((OUT_TM, SSM_WIDTH), row),
            pl.BlockSpec((OUT_TM, D_MODEL), row),
            pl.BlockSpec((ATTN_WIDTH, D_MODEL), const),
            pl.BlockSpec((SSM_WIDTH, D_MODEL), const),
            pl.BlockSpec((1, D_MODEL), const),
            pl.BlockSpec((D_MODEL, ROUTE_LANES), const),
            pl.BlockSpec((1, ROUTE_LANES), const),
        ],
        out_specs=(pl.BlockSpec((OUT_TM, D_MODEL), row), pl.BlockSpec((OUT_TM, D_MODEL), row),
                   pl.BlockSpec((OUT_TM, ROUTE_LANES), row), pl.BlockSpec((OUT_TM, ROUTE_LANES), row)),
        compiler_params=_cparams(("parallel",), 48),
        name="outproj",
    )(attn, ssd, x, wa, ws, nw, wr, br)


def _row_copy(src_hbm, row, dst, r, sem):
    return pltpu.make_async_copy(src_hbm.at[pl.ds(row, 1), :], dst.at[pl.ds(r, 1), :], sem)


def _gather_kernel(nused_ref, tok_ref, src_hbm, out_ref, idx_smem, buf, sem, isem):
    i = pl.program_id(0)

    @pl.when(i < nused_ref[0])
    def _():
        cp = pltpu.make_async_copy(tok_ref.at[0], idx_smem, isem)
        cp.start()
        cp.wait()

        def issue(r, carry):
            _row_copy(src_hbm, idx_smem[0, r], buf, r, sem).start()
            return carry

        def drain(r, carry):
            _row_copy(src_hbm, 0, buf, r, sem).wait()
            return carry

        lax.fori_loop(0, MOE_TM, issue, 0, unroll=8)
        lax.fori_loop(0, MOE_TM, drain, 0, unroll=8)
        out_ref[...] = buf[...].astype(BF16)

    @pl.when(i >= nused_ref[0])
    def _():
        out_ref[...] = jnp.zeros_like(out_ref)


def _gather(h2, slot_tok, n_used):
    n_blocks = slot_tok.shape[0]
    grid_spec = pltpu.PrefetchScalarGridSpec(
        num_scalar_prefetch=1,
        grid=(n_blocks,),
        in_specs=[pl.BlockSpec((1, 1, MOE_TM), lambda i, *_: (i, 0, 0)),
                  pl.BlockSpec(memory_space=pl.ANY)],
        out_specs=pl.BlockSpec((MOE_TM, D_MODEL), lambda i, *_: (i, 0)),
        scratch_shapes=[pltpu.SMEM((1, MOE_TM), jnp.int32), pltpu.VMEM((MOE_TM, D_MODEL), F32),
                        pltpu.SemaphoreType.DMA, pltpu.SemaphoreType.DMA],
    )
    return pl.pallas_call(
        _gather_kernel,
        out_shape=jax.ShapeDtypeStruct((n_blocks * MOE_TM, D_MODEL), BF16),
        grid_spec=grid_spec,
        compiler_params=_cparams(("arbitrary",), 16),
        name="moe_gather",
    )(n_used, slot_tok, h2)


def _expert_kernel(be_ref, nused_ref, x_ref, wg_ref, wu_ref, wd_ref, sw_ref, out_ref):
    i = pl.program_id(0)

    @pl.when(i < nused_ref[0])
    def _():
        x = x_ref[...]
        g = jnp.dot(x, wg_ref[...], preferred_element_type=F32)
        u = jnp.dot(x, wu_ref[...], preferred_element_type=F32)
        hmid = (_silu(g) * u).astype(BF16)
        y = jnp.dot(hmid, wd_ref[...], preferred_element_type=F32)
        out_ref[...] = y * sw_ref[...]

    @pl.when(i >= nused_ref[0])
    def _():
        out_ref[...] = jnp.zeros_like(out_ref)


def _experts(xs, block_expert, n_used, wg, wu, wd, slot_w):
    n_slots = xs.shape[0]
    n_blocks = n_slots // MOE_TM
    blk = lambda i, be, nu: (jnp.minimum(i, nu[0] - 1), 0)
    wmap = lambda i, be, nu: (be[i], 0, 0)
    grid_spec = pltpu.PrefetchScalarGridSpec(
        num_scalar_prefetch=2,
        grid=(n_blocks,),
        in_specs=[
            pl.BlockSpec((MOE_TM, D_MODEL), blk),
            pl.BlockSpec((None, D_MODEL, D_FF_EXPERT), wmap),
            pl.BlockSpec((None, D_MODEL, D_FF_EXPERT), wmap),
            pl.BlockSpec((None, D_FF_EXPERT, D_MODEL), wmap),
            pl.BlockSpec((MOE_TM, 1), blk),
        ],
        out_specs=pl.BlockSpec((MOE_TM, D_MODEL), lambda i, be, nu: (i, 0)),
    )
    return pl.pallas_call(
        _expert_kernel,
        out_shape=jax.ShapeDtypeStruct((n_slots, D_MODEL), F32),
        grid_spec=grid_spec,
        compiler_params=_cparams(("arbitrary",), 48),
        name="moe_experts",
    )(block_expert, n_used, xs, wg, wu, wd, slot_w)


def _combine_kernel(idx_ref, x1_ref, ys_hbm, nw_ref, out_ref, idx_smem, buf_a, buf_b, sem, isem, *, final_norm):
    cp = pltpu.make_async_copy(idx_ref.at[0], idx_smem, isem)
    cp.start()
    cp.wait()

    def issue(r, carry):
        _row_copy(ys_hbm, idx_smem[0, r], buf_a, r, sem).start()
        _row_copy(ys_hbm, idx_smem[0, COMB_TM + r], buf_b, r, sem).start()
        return carry

    def drain(r, carry):
        _row_copy(ys_hbm, 0, buf_a, r, sem).wait()
        _row_copy(ys_hbm, 0, buf_b, r, sem).wait()
        return carry

    lax.fori_loop(0, COMB_TM, issue, 0, unroll=8)
    lax.fori_loop(0, COMB_TM, drain, 0, unroll=8)
    o = x1_ref[...] + (buf_a[...] + buf_b[...])
    if final_norm:
        o = _rms(o, nw_ref[...])
    out_ref[...] = o


def _combine(x1, ys, dest_idx, nw, *, final_norm):
    t = x1.shape[0]
    assert t % COMB_TM == 0
    return pl.pallas_call(
        functools.partial(_combine_kernel, final_norm=final_norm),
        out_shape=jax.ShapeDtypeStruct((t, D_MODEL), F32),
        grid=(t // COMB_TM,),
        in_specs=[pl.BlockSpec((1, 1, 2 * COMB_TM), lambda i: (i, 0, 0)),
                  pl.BlockSpec((COMB_TM, D_MODEL), lambda i: (i, 0)),
                  pl.BlockSpec(memory_space=pl.ANY),
                  pl.BlockSpec((1, D_MODEL), lambda i: (0, 0))],
        out_specs=pl.BlockSpec((COMB_TM, D_MODEL), lambda i: (i, 0)),
        scratch_shapes=[pltpu.SMEM((1, 2 * COMB_TM), jnp.int32), pltpu.VMEM((COMB_TM, D_MODEL), F32),
                        pltpu.VMEM((COMB_TM, D_MODEL), F32), pltpu.SemaphoreType.DMA, pltpu.SemaphoreType.DMA],
        compiler_params=_cparams(("arbitrary",), 24),
        name="moe_combine",
    )(dest_idx, x1, ys, nw)


def _dispatch_plan(eid, gate):
    t = eid.shape[0]
    n_assign = t * TOP_K
    e_flat = eid.reshape(-1)
    onehot = (e_flat[:, None] == jnp.arange(N_EXPERTS, dtype=jnp.int32)[None, :]).astype(jnp.int32)
    csum = jnp.cumsum(onehot, axis=0)
    counts = csum[-1]
    rank = jnp.take_along_axis(csum, e_flat[:, None], axis=1)[:, 0] - 1
    padded = (counts + MOE_TM - 1) // MOE_TM * MOE_TM
    pend = jnp.cumsum(padded)
    pstart = pend - padded
    dest = pstart[e_flat] + rank
    n_blocks = -(-n_assign // MOE_TM) + N_EXPERTS
    n_slots = n_blocks * MOE_TM
    tok_flat = jnp.arange(n_assign, dtype=jnp.int32) // TOP_K
    slot_tok = jnp.zeros((n_slots,), jnp.int32).at[dest].set(tok_flat)
    slot_w = jnp.zeros((n_slots,), F32).at[dest].set(gate.reshape(-1))
    block_expert = jnp.minimum(
        jnp.searchsorted(pend, jnp.arange(n_blocks, dtype=jnp.int32) * MOE_TM, side='right'),
        N_EXPERTS - 1).astype(jnp.int32)
    n_used = (pend[-1] // MOE_TM).astype(jnp.int32).reshape(1)
    dest_idx = dest.reshape(t // COMB_TM, COMB_TM, TOP_K).transpose(0, 2, 1).reshape(t // COMB_TM, 1,
                                                                                       TOP_K * COMB_TM)
    return (slot_tok.reshape(n_blocks, 1, MOE_TM), slot_w.reshape(n_slots, 1), block_expert, n_used,
            dest_idx.astype(jnp.int32))


def _seq_flags(seq_lens, tile):
    first, last = [], []
    for s in seq_lens:
        assert s % tile == 0
        n = s // tile
        first += [1] + [0] * (n - 1)
        last += [0] * (n - 1) + [1]
    return jnp.asarray(first, jnp.int32), jnp.asarray(last, jnp.int32)


def _rope_tables(seq_lens):
    pos = jnp.concatenate([jnp.arange(s, dtype=F32) for s in seq_lens])
    inv_freq = ROPE_THETA ** (-jnp.arange(0, ROPE_DIM, 2, dtype=F32) / ROPE_DIM)
    ang = pos[:, None] * inv_freq[None, :]
    d = jnp.arange(LANES) % HEAD_DIM
    cos = jnp.cos(ang)[:, d % ROPE_HALF]
    sin = jnp.sin(ang)[:, d % ROPE_HALF]
    c = jnp.where(d < ROPE_DIM, cos, 1.0)
    sa = jnp.where((d >= ROPE_HALF) & (d < ROPE_DIM), sin, 0.0)
    sb = jnp.where(d < ROPE_HALF, -sin, 0.0)
    return c, sa, sb


def _head_expand(d0):
    rows = jnp.arange(LANES)[:, None]
    cols = jnp.arange(SSM_WIDTH)[None, :] // SSM_HEAD_DIM
    return (rows == cols + d0).astype(BF16)


def _pad_lanes(v, width=LANES):
    v = v.reshape(1, -1)
    return jnp.pad(v, ((0, 0), (0, width - v.shape[1])))


def _trunk(x, seq_lens, attn_norm_w, w_in, conv_w, conv_b, attn_sink, attn_out_norm_w, ssm_a_log, ssm_dt_bias,
           ssm_d, ssm_norm_w, w_out, ffn_norm_w, w_router_group, b_router_group, w_router_expert,
           b_router_expert, w_gate, w_up, w_down, final_norm_w):
    depth = w_in.shape[0]
    rope_c, rope_sa, rope_sb = _rope_tables(seq_lens)
    tile_first, tile_last = _seq_flags(seq_lens, ATTN_TQ)
    chunk_first, chunk_last = _seq_flags(seq_lens, CHUNK)
    e_fwd, e_bwd = _head_expand(0), _head_expand(SSM_HEADS)
    z_end = ATTN_WIDTH + 2 * KV_WIDTH + SSM_WIDTH
    for l in range(depth):
        w = w_in[l]
        w_main = jnp.concatenate([w[:, :ATTN_WIDTH + 2 * KV_WIDTH], w[:, z_end:z_end + CONV_CH],
                                  w[:, ATTN_WIDTH + 2 * KV_WIDTH:z_end]], axis=1).astype(BF16)
        w_dt = jnp.pad(w[:, z_end + CONV_CH:], ((0, 0), (0, LANES - 2 * SSM_HEADS))).astype(BF16)
        proj, dt_raw = _proj(x, attn_norm_w[l].reshape(1, -1), w_main, w_dt, rope_c, rope_sa, rope_sb)
        attn = _attention(proj, tile_first, tile_last, attn_sink[l], attn_out_norm_w[l].reshape(1, -1))
        cw = jnp.pad(conv_w[l], ((0, 8 - CONV_WIDTH), (0, 0)))
        cb = conv_b[l].reshape(1, -1)
        dtb = _pad_lanes(ssm_dt_bias[l])
        alog = _pad_lanes(ssm_a_log[l])
        dexp = jnp.repeat(ssm_d[l], SSM_HEAD_DIM).reshape(1, -1)
        y_fwd = _ssd(proj, dt_raw, chunk_first, chunk_last, cw, cb, dtb, alog, e_fwd, dexp, backward=False)
        ssd = _ssd(proj, dt_raw, chunk_first, chunk_last, cw, cb, dtb, alog, e_bwd,
                   ssm_norm_w[l].reshape(1, -1), backward=True, y_fwd=y_fwd)
        wo = w_out[l].astype(BF16)
        wr = jnp.pad(jnp.concatenate([w_router_group[l], w_router_expert[l]], axis=1),
                     ((0, 0), (0, ROUTE_LANES - N_EXPERT_GROUPS - N_EXPERTS)))
        br = _pad_lanes(jnp.concatenate([b_router_group[l], b_router_expert[l]]), ROUTE_LANES)
        x1, h2, eid, gate = _outproj(attn, ssd, x, wo[:ATTN_WIDTH], wo[ATTN_WIDTH:], ffn_norm_w[l].reshape(1, -1),
                                     wr, br)
        slot_tok, slot_w, block_expert, n_used, dest_idx = _dispatch_plan(eid[:, :TOP_K], gate[:, :TOP_K])
        xs = _gather(h2, slot_tok, n_used)
        ys = _experts(xs, block_expert, n_used, w_gate[l].astype(BF16), w_up[l].astype(BF16),
                      w_down[l].astype(BF16), slot_w)
        x = _combine(x1, ys, dest_idx, final_norm_w.reshape(1, -1), final_norm=(l == depth - 1))
    return x


def kernel(x_prompt, x_sample, attn_norm_w, w_in, conv_w, conv_b, attn_sink, attn_out_norm_w, ssm_a_log,
           ssm_dt_bias, ssm_d, ssm_norm_w, w_out, ffn_norm_w, w_router_group, b_router_group, w_router_expert,
           b_router_expert, w_gate, w_up, w_down, final_norm_w):
    bp, sp, d = x_prompt.shape
    bs, ss, _ = x_sample.shape
    seq_lens = [sp] * bp + [ss] * bs
    x = jnp.concatenate([x_prompt.reshape(bp * sp, d), x_sample.reshape(bs * ss, d)], axis=0)
    y = _trunk(x, seq_lens, attn_norm_w, w_in, conv_w, conv_b, attn_sink, attn_out_norm_w, ssm_a_log,
               ssm_dt_bias, ssm_d, ssm_norm_w, w_out, ffn_norm_w, w_router_group, b_router_group,
               w_router_expert, b_router_expert, w_gate, w_up, w_down, final_norm_w)
    return (y[:bp * sp].reshape(bp, sp, d), y[bp * sp:].reshape(bs, ss, d))
```

```python
import functools
import math

import numpy as np
import jax
import jax.numpy as jnp
from jax import lax
from jax.experimental import pallas as pl
from jax.experimental.pallas import tpu as pltpu

D_MODEL = 2048
N_Q_HEADS = 16
N_KV_HEADS = 4
HEAD_DIM = 64
Q_GROUP = N_Q_HEADS // N_KV_HEADS
ATTN_WIDTH = N_Q_HEADS * HEAD_DIM
KV_WIDTH = N_KV_HEADS * HEAD_DIM
WINDOW = 128
BLOCK = 128
ROPE_THETA = 500000.0
ROPE_DIM = HEAD_DIM // 4
ROPE_HALF = ROPE_DIM // 2
SSM_HEADS = 16
SSM_HEAD_DIM = 64
SSM_WIDTH = SSM_HEADS * SSM_HEAD_DIM
SSM_GROUPS = 2
HEADS_PER_GROUP = SSM_HEADS // SSM_GROUPS
D_STATE = 128
BC_WIDTH = SSM_GROUPS * D_STATE
CONV_WIDTH = 5
CONV_PAD = CONV_WIDTH // 2
CONV_CH = SSM_WIDTH + 2 * BC_WIDTH
CHUNK = 128
N_EXPERT_GROUPS = 4
EXPERTS_PER_GROUP = 8
N_EXPERTS = N_EXPERT_GROUPS * EXPERTS_PER_GROUP
TOP_K = 2
D_FF_EXPERT = 1024
EPS = 1e-6

LANES = 128
SUBLANES = 8
BF16_SUBLANES = 16
MIB = 1024 * 1024

COL_Q = 0
COL_K = ATTN_WIDTH
COL_V = COL_K + KV_WIDTH
COL_XBC = COL_V + KV_WIDTH
COL_Z = COL_XBC + CONV_CH
PROJ_MAIN = COL_Z + SSM_WIDTH
PROJ_TM = 512
PROJ_TN = 1024
ATTN_TQ = 256
OUT_TM = 256
MOE_TM = 256
ROW_TM = 256
HALF = D_MODEL // 2
CONV_K = 2 * CHUNK
NEG = -1e30
LOG2E = math.log2(math.e)
Q_SCALE = HEAD_DIM ** -0.5 * LOG2E
F32 = jnp.float32
BF16 = jnp.bfloat16
U32 = jnp.uint32
I32 = jnp.int32


def _cparams(semantics, vmem_mib):
    return pltpu.CompilerParams(dimension_semantics=semantics, vmem_limit_bytes=vmem_mib * MIB)


def _rms(x, w):
    ms = jnp.mean(x * x, axis=-1, keepdims=True)
    return (x * lax.rsqrt(ms + EPS)) * w


def _silu(x):
    return x / (1.0 + jnp.exp(-x))


def _split2(v):
    hi = v.astype(BF16)
    lo = (v - hi.astype(F32)).astype(BF16)
    return hi, lo


def _split3(v):
    hi = v.astype(BF16)
    r = v - hi.astype(F32)
    mid = r.astype(BF16)
    lo = (r - mid.astype(F32)).astype(BF16)
    return hi, mid, lo


def _pack_pair(lo, hi):
    lo_b = lax.bitcast_convert_type(lo.astype(BF16).astype(F32), U32)
    hi_b = lax.bitcast_convert_type(hi.astype(BF16).astype(F32), U32)
    return (lo_b >> 16) | (hi_b & jnp.uint32(0xFFFF0000))


def _unpack_pair(w):
    lo = lax.bitcast_convert_type(w << 16, F32)
    hi = lax.bitcast_convert_type(w & jnp.uint32(0xFFFF0000), F32)
    return lo, hi


def _select_src(i, x_refs, src_tiles):
    if len(x_refs) == 1:
        return x_refs[0][...]
    return jnp.where(i < src_tiles, x_refs[0][...], x_refs[1][...])


def _src_specs(xs, tm, width, nargs):
    if len(xs) == 1:
        return [pl.BlockSpec((tm, width), lambda i, *_: (i, 0))], 0
    n0 = xs[0].shape[0] // tm
    return [pl.BlockSpec((tm, width), lambda i, *_: (jnp.minimum(i, n0 - 1), 0)),
            pl.BlockSpec((tm, width), lambda i, *_: (jnp.maximum(i - n0, 0), 0))], n0


def _proj_kernel(*refs, n_src, src_tiles):
    x_refs = refs[:n_src]
    nw_ref, w_ref, wdt_ref, c_ref, sa_ref, sb_ref, out_ref, dt_ref, h_ref = refs[n_src:]
    i = pl.program_id(0)
    j = pl.program_id(1)

    @pl.when(j == 0)
    def _():
        h = _rms(_select_src(i, x_refs, src_tiles), nw_ref[...]).astype(BF16)
        h_ref[...] = h
        dt_ref[...] = jnp.dot(h, wdt_ref[...], preferred_element_type=F32)

    acc = jnp.dot(h_ref[...], w_ref[...], preferred_element_type=F32)

    def rope(a):
        return (a * c_ref[...] + pltpu.roll(a, ROPE_HALF, 1) * sa_ref[...]
                + pltpu.roll(a, LANES - ROPE_HALF, 1) * sb_ref[...])

    def store(n_rope_tiles, scale):
        for cb in range(PROJ_TN // LANES):
            a = acc[:, cb * LANES:(cb + 1) * LANES]
            if cb < n_rope_tiles:
                a = rope(a)
            if scale is not None:
                a = a * scale
            out_ref[:, cb * LANES:(cb + 1) * LANES] = a.astype(BF16)

    @pl.when(j == COL_Q // PROJ_TN)
    def _():
        store(ATTN_WIDTH // LANES, Q_SCALE)

    @pl.when(j == COL_K // PROJ_TN)
    def _():
        store(KV_WIDTH // LANES, None)

    @pl.when(j > COL_K // PROJ_TN)
    def _():
        store(0, None)


def _proj(xs, nw, w_main, w_dt, rope_c, rope_sa, rope_sb):
    t = sum(x.shape[0] for x in xs)
    assert t % PROJ_TM == 0 and all(x.shape[0] % PROJ_TM == 0 for x in xs)
    row = lambda i, j: (i, 0)
    const = lambda i, j: (0, 0)
    src_specs, n0 = _src_specs(xs, PROJ_TM, D_MODEL, 2)
    return pl.pallas_call(
        functools.partial(_proj_kernel, n_src=len(xs), src_tiles=n0),
        out_shape=(jax.ShapeDtypeStruct((t, PROJ_MAIN), BF16), jax.ShapeDtypeStruct((t, LANES), F32)),
        grid=(t // PROJ_TM, PROJ_MAIN // PROJ_TN),
        in_specs=src_specs + [
            pl.BlockSpec((1, D_MODEL), const),
            pl.BlockSpec((D_MODEL, PROJ_TN), lambda i, j: (0, j)),
            pl.BlockSpec((D_MODEL, LANES), const),
            pl.BlockSpec((PROJ_TM, LANES), row),
            pl.BlockSpec((PROJ_TM, LANES), row),
            pl.BlockSpec((PROJ_TM, LANES), row),
        ],
        out_specs=(pl.BlockSpec((PROJ_TM, PROJ_TN), lambda i, j: (i, j)),
                   pl.BlockSpec((PROJ_TM, LANES), row)),
        scratch_shapes=[pltpu.VMEM((PROJ_TM, D_MODEL), BF16)],
        compiler_params=_cparams(("arbitrary", "arbitrary"), 40),
        name="proj",
    )(*xs, nw, w_main, w_dt, rope_c, rope_sa, rope_sb)


def _attn_kernel(tf_ref, tl_ref, sink_ref, q_ref, kc_ref, vc_ref, kp_ref, vp_ref, kn_ref, vn_ref,
                 nw_ref, o_ref):
    i = pl.program_id(0)
    kw = jnp.concatenate([kp_ref[...], kc_ref[...], kn_ref[...]], axis=0)
    vw = jnp.concatenate([vp_ref[...], vc_ref[...], vn_ref[...]], axis=0)
    rows = kw.shape[0]
    lane_w = lax.broadcasted_iota(I32, (rows, LANES), 1)
    low = lane_w < HEAD_DIM
    one_e = jnp.where(lane_w == 0, 1.0, 0.0).astype(BF16)
    one_o = jnp.where(lane_w == 1, 1.0, 0.0).astype(BF16)

    def pad_pair(slab, g):
        s = slab.astype(F32)
        if g % 2 == 0:
            e = jnp.where(low, s, 0.0)
            o = pltpu.roll(e, HEAD_DIM, 1)
        else:
            o = jnp.where(low, 0.0, s)
            e = pltpu.roll(o, HEAD_DIM, 1)
        return e.astype(BF16), o.astype(BF16)

    ke, ko, ve, vo = [], [], [], []
    for g in range(N_KV_HEADS):
        c0 = (g // 2) * LANES
        e, o = pad_pair(kw[:, c0:c0 + LANES], g)
        ke.append(e)
        ko.append(o)
        e, o = pad_pair(vw[:, c0:c0 + LANES], g)
        ve.append(jnp.concatenate([e, one_e], axis=1))
        vo.append(jnp.concatenate([o, one_o], axis=1))

    qi = lax.broadcasted_iota(I32, (BLOCK, BLOCK), 0)
    kk = lax.broadcasted_iota(I32, (BLOCK, BLOCK), 1)
    bias_prev = jnp.where(kk >= qi, 0.0, NEG)
    bias_next = jnp.where(kk <= qi, 0.0, NEG)
    lane_b = lax.broadcasted_iota(I32, (BLOCK, LANES), 1)
    nsb = ATTN_TQ // BLOCK
    for sb in range(nsb):
        bp, bn = bias_prev, bias_next
        if sb == 0:
            bp = jnp.where(tf_ref[i] == 1, NEG, bias_prev)
        if sb == nsb - 1:
            bn = jnp.where(tl_ref[i] == 1, NEG, bias_next)
        r0 = sb * BLOCK
        slabs = []
        for g in range(N_KV_HEADS):
            q0 = g * Q_GROUP * HEAD_DIM
            lhs = jnp.concatenate([q_ref[r0:r0 + BLOCK, q0:q0 + LANES],
                                   q_ref[r0:r0 + BLOCK, q0 + LANES:q0 + 2 * LANES]], axis=0)
            rhs = jnp.concatenate([ke[g][r0:r0 + 3 * BLOCK], ko[g][r0:r0 + 3 * BLOCK]], axis=0)
            s = lax.dot_general(lhs, rhs, (((1,), (1,)), ((), ())), preferred_element_type=F32)
            rhs_v = jnp.concatenate([ve[g][r0:r0 + 3 * BLOCK], vo[g][r0:r0 + 3 * BLOCK]], axis=0)
            for half in range(2):
                ps, ms = [], []
                for par in range(2):
                    h = g * Q_GROUP + 2 * half + par
                    c0 = par * 3 * BLOCK
                    sq = s[half * BLOCK:(half + 1) * BLOCK, c0:c0 + 3 * BLOCK]
                    s0 = sq[:, :BLOCK] + bp
                    s1 = sq[:, BLOCK:2 * BLOCK]
                    s2 = sq[:, 2 * BLOCK:] + bn
                    m = jnp.max(jnp.maximum(jnp.maximum(s0, s1), s2), axis=-1, keepdims=True)
                    m = jnp.maximum(m, sink_ref[h])
                    ps += [jnp.exp2(s0 - m).astype(BF16), jnp.exp2(s1 - m).astype(BF16),
                           jnp.exp2(s2 - m).astype(BF16)]
                    ms.append(m)
                o2 = jnp.dot(jnp.concatenate(ps, axis=1), rhs_v, preferred_element_type=F32)
                h_e = g * Q_GROUP + 2 * half
                d_e = o2[:, LANES:LANES + 1] + jnp.exp2(sink_ref[h_e] - ms[0])
                d_o = o2[:, LANES + 1:LANES + 2] + jnp.exp2(sink_ref[h_e + 1] - ms[1])
                inv = jnp.where(lane_b < HEAD_DIM, 1.0 / d_e, 1.0 / d_o)
                slabs.append(o2[:, :LANES] * inv)
        o = jnp.concatenate(slabs, axis=1)
        o_ref[r0:r0 + BLOCK, :] = _rms(o, nw_ref[...]).astype(BF16)


def _attention(proj, tile_first, tile_last, sink2, nw):
    t = proj.shape[0]
    assert t % ATTN_TQ == 0
    nb = t // BLOCK
    r = ATTN_TQ // BLOCK
    kcol, vcol = COL_K // KV_WIDTH, COL_V // KV_WIDTH
    prev = lambda i: jnp.maximum(i * r - 1, 0)
    nxt = lambda i: jnp.minimum(i * r + r, nb - 1)
    grid_spec = pltpu.PrefetchScalarGridSpec(
        num_scalar_prefetch=3,
        grid=(t // ATTN_TQ,),
        in_specs=[
            pl.BlockSpec((ATTN_TQ, ATTN_WIDTH), lambda i, *_: (i, COL_Q // ATTN_WIDTH)),
            pl.BlockSpec((ATTN_TQ, KV_WIDTH), lambda i, *_: (i, kcol)),
            pl.BlockSpec((ATTN_TQ, KV_WIDTH), lambda i, *_: (i, vcol)),
            pl.BlockSpec((BLOCK, KV_WIDTH), lambda i, *_: (prev(i), kcol)),
            pl.BlockSpec((BLOCK, KV_WIDTH), lambda i, *_: (prev(i), vcol)),
            pl.BlockSpec((BLOCK, KV_WIDTH), lambda i, *_: (nxt(i), kcol)),
            pl.BlockSpec((BLOCK, KV_WIDTH), lambda i, *_: (nxt(i), vcol)),
            pl.BlockSpec((1, ATTN_WIDTH), lambda i, *_: (0, 0)),
        ],
        out_specs=pl.BlockSpec((ATTN_TQ, ATTN_WIDTH), lambda i, *_: (i, 0)),
    )
    return pl.pallas_call(
        _attn_kernel,
        out_shape=jax.ShapeDtypeStruct((t, ATTN_WIDTH), BF16),
        grid_spec=grid_spec,
        compiler_params=_cparams(("arbitrary",), 32),
        name="attn",
    )(tile_first, tile_last, sink2, proj, proj, proj, proj, proj, proj, proj, nw)


def _ssd_kernel(first_ref, last_ref, xc_ref, xp_ref, xn_ref, dt_ref, *rest, backward):
    if backward:
        (z_ref, yf_ref, sh_ref, cw_ref, cb_ref, dtb_ref, alog_ref, e_ref, nw_ref, out_ref, h_ref) = rest
    else:
        (sh_ref, cw_ref, cb_ref, dtb_ref, alog_ref, e_ref, dexp_ref, out_ref, h_ref) = rest
    i = pl.program_id(0)
    c = (pl.num_programs(0) - 1 - i) if backward else i
    is_first = first_ref[c] == 1
    is_last = last_ref[c] == 1

    @pl.when(is_last if backward else is_first)
    def _():
        h_ref[...] = jnp.zeros_like(h_ref)

    halo = BF16_SUBLANES
    xc = xc_ref[...]
    no_rows = jnp.zeros((halo, CONV_CH), BF16)
    xp = jnp.where(is_first, no_rows, xp_ref[...])
    xn = jnp.where(is_last, no_rows, xn_ref[...])
    xw = jnp.concatenate([xc, xp, xn, jnp.zeros((CONV_K - CHUNK - 2 * halo, CONV_CH), BF16)], axis=0)
    shifted = jnp.dot(sh_ref[...], xw, preferred_element_type=F32)
    acc = cb_ref[...] + xc.astype(F32) * cw_ref[CONV_PAD:CONV_PAD + 1, :]
    taps = [k for k in range(CONV_WIDTH) if k != CONV_PAD]
    for n, k in enumerate(taps):
        acc = acc + shifted[n * CHUNK:(n + 1) * CHUNK] * cw_ref[k:k + 1, :]
    u = _silu(acc)
    xs = u[:, :SSM_WIDTH]
    bm = u[:, SSM_WIDTH:SSM_WIDTH + BC_WIDTH]
    cm = u[:, SSM_WIDTH + BC_WIDTH:]

    d0 = SSM_HEADS if backward else 0
    draw = dt_ref[...] + dtb_ref[...]
    dt = jnp.maximum(draw, 0.0) + jnp.log1p(jnp.exp(-jnp.abs(draw)))
    a = dt * (-jnp.exp(alog_ref[...]))
    li = lax.broadcasted_iota(I32, (CHUNK, CHUNK), 0)
    ti = lax.broadcasted_iota(I32, (CHUNK, CHUNK), 1)
    causal = (li <= ti) if backward else (li >= ti)
    tri = jnp.where(causal, 1.0, 0.0).astype(BF16)
    a_hi, a_mid, a_lo = _split3(a)
    acum = (jnp.dot(tri, a_hi, preferred_element_type=F32) + jnp.dot(tri, a_mid, preferred_element_type=F32)
            + jnp.dot(tri, a_lo, preferred_element_type=F32))
    end = 0 if backward else CHUNK - 1
    tot = acum[end:end + 1, :]
    exp_a = jnp.exp(acum)
    wdec = dt * jnp.exp(tot - acum)
    stack = jnp.concatenate(_split2(dt) + _split2(exp_a) + _split2(wdec), axis=0)
    ex = jnp.dot(stack, e_ref[...], preferred_element_type=F32)
    dt_e = ex[0:CHUNK] + ex[CHUNK:2 * CHUNK]
    ea_e = ex[2 * CHUNK:3 * CHUNK] + ex[3 * CHUNK:4 * CHUNK]
    wd_e = ex[4 * CHUNK:5 * CHUNK] + ex[5 * CHUNK:6 * CHUNK]
    xdt = xs * dt_e
    xd = (xs * wd_e).astype(BF16)
    acum_t = acum.T
    hprev = h_ref[...]
    hb = hprev.astype(BF16)
    lane = lax.broadcasted_iota(I32, (CHUNK, LANES), 1)
    gw = HEADS_PER_GROUP * SSM_HEAD_DIM
    ys = []
    sts = []
    for g in range(SSM_GROUPS):
        bg = bm[:, g * D_STATE:(g + 1) * D_STATE]
        cg = cm[:, g * D_STATE:(g + 1) * D_STATE].astype(BF16)
        cb = lax.dot_general(cg, bg.astype(BF16), (((1,), (1,)), ((), ())), preferred_element_type=F32)
        yoff = jnp.dot(cg, hb[:, g * gw:(g + 1) * gw], preferred_element_type=F32)
        sts.append(jnp.dot(bg.T.astype(BF16), xd[:, g * gw:(g + 1) * gw], preferred_element_type=F32))
        for pr in range(HEADS_PER_GROUP // 2):
            h0 = g * HEADS_PER_GROUP + 2 * pr
            ms = []
            for hh in (h0, h0 + 1):
                col = acum[:, d0 + hh:d0 + hh + 1]
                row = acum_t[d0 + hh:d0 + hh + 1, :]
                decay = jnp.exp(jnp.where(causal, col - row, NEG))
                ms.append((cb * decay).astype(BF16))
            lhs = jnp.concatenate(ms, axis=1)
            c0 = (h0 // 2) * LANES
            slab = xdt[:, c0:c0 + LANES]
            rhs = jnp.concatenate([jnp.where(lane < SSM_HEAD_DIM, slab, 0.0),
                                   jnp.where(lane >= SSM_HEAD_DIM, slab, 0.0)], axis=0).astype(BF16)
            yd = jnp.dot(lhs, rhs, preferred_element_type=F32)
            ys.append(yd + yoff[:, pr * LANES:(pr + 1) * LANES] * ea_e[:, c0:c0 + LANES])
    y = jnp.concatenate(ys, axis=1)
    h_ref[...] = hprev * ea_e[end:end + 1, :] + jnp.concatenate(sts, axis=1)

    if backward:
        yt = yf_ref[...] + y
        gz = yt * _silu(z_ref[...].astype(F32))
        out_ref[...] = _rms(gz, nw_ref[...]).astype(BF16)
    else:
        out_ref[...] = y + xs * dexp_ref[...]


def _conv_shift_matrix():
    halo = BF16_SUBLANES
    taps = [k for k in range(CONV_WIDTH) if k != CONV_PAD]
    m = np.zeros((len(taps) * CHUNK, CONV_K), np.float32)
    for n, k in enumerate(taps):
        for t in range(CHUNK):
            j = t + k - CONV_PAD
            if j < 0:
                col = CHUNK + halo + j
            elif j >= CHUNK:
                col = CHUNK + halo + (j - CHUNK)
            else:
                col = j
            m[n * CHUNK + t, col] = 1.0
    return jnp.asarray(m, BF16)


def _ssd(proj, dt_raw, chunk_first, chunk_last, shift, conv_w, conv_b, dt_bias, a_log, expand, extra, *,
         backward, y_fwd=None):
    t = proj.shape[0]
    nc = t // CHUNK
    hb = CHUNK // BF16_SUBLANES
    n_halo = t // BF16_SUBLANES
    xcol = COL_XBC // CONV_CH
    ch = (lambda i: nc - 1 - i) if backward else (lambda i: i)
    const = lambda i, *_: (0, 0)
    in_specs = [
        pl.BlockSpec((CHUNK, CONV_CH), lambda i, *_: (ch(i), xcol)),
        pl.BlockSpec((BF16_SUBLANES, CONV_CH), lambda i, *_: (jnp.maximum(ch(i) * hb - 1, 0), xcol)),
        pl.BlockSpec((BF16_SUBLANES, CONV_CH), lambda i, *_: (jnp.minimum(ch(i) * hb + hb, n_halo - 1), xcol)),
        pl.BlockSpec((CHUNK, LANES), lambda i, *_: (ch(i), 0)),
    ]
    args = [proj, proj, proj, dt_raw]
    if backward:
        in_specs += [pl.BlockSpec((CHUNK, SSM_WIDTH), lambda i, *_: (ch(i), COL_Z // SSM_WIDTH)),
                     pl.BlockSpec((CHUNK, SSM_WIDTH), lambda i, *_: (ch(i), 0))]
        args += [proj, y_fwd]
    in_specs += [
        pl.BlockSpec(shift.shape, const),
        pl.BlockSpec((SUBLANES, CONV_CH), const),
        pl.BlockSpec((1, CONV_CH), const),
        pl.BlockSpec((1, LANES), const),
        pl.BlockSpec((1, LANES), const),
        pl.BlockSpec((LANES, SSM_WIDTH), const),
        pl.BlockSpec((1, SSM_WIDTH), const),
    ]
    args += [shift, conv_w, conv_b, dt_bias, a_log, expand, extra]
    grid_spec = pltpu.PrefetchScalarGridSpec(
        num_scalar_prefetch=2,
        grid=(nc,),
        in_specs=in_specs,
        out_specs=pl.BlockSpec((CHUNK, SSM_WIDTH), lambda i, *_: (ch(i), 0)),
        scratch_shapes=[pltpu.VMEM((D_STATE, SSM_WIDTH), F32)],
    )
    return pl.pallas_call(
        functools.partial(_ssd_kernel, backward=backward),
        out_shape=jax.ShapeDtypeStruct((t, SSM_WIDTH), BF16 if backward else F32),
        grid_spec=grid_spec,
        compiler_params=_cparams(("arbitrary",), 32),
        name="ssd_bwd" if backward else "ssd_fwd",
    )(chunk_first, chunk_last, *args)


def _outproj_kernel(*refs, n_src, src_tiles):
    a_ref, s_ref = refs[:2]
    x_refs = refs[2:2 + n_src]
    (wa_ref, ws_ref, nw_ref, wr1_ref, wr2_ref, br_ref, tri_ref, x1_ref, h2_ref, eid_ref, gate_ref, cnt_ref,
     run_ref) = refs[2 + n_src:]
    i = pl.program_id(0)

    @pl.when(i == 0)
    def _():
        run_ref[...] = jnp.zeros_like(run_ref)

    y = (jnp.dot(a_ref[...], wa_ref[...], preferred_element_type=F32)
         + jnp.dot(s_ref[...], ws_ref[...], preferred_element_type=F32))
    x1 = _select_src(i, x_refs, src_tiles) + y
    x1_ref[...] = x1
    h2 = _rms(x1, nw_ref[...])
    h2_ref[...] = _pack_pair(h2[:, :HALF], h2[:, HALF:])
    h_hi, h_lo = _split2(h2)
    part = jnp.dot(h_hi, wr1_ref[...], preferred_element_type=F32)
    logits = (part[:, :LANES] + (part[:, LANES:] + jnp.dot(h_lo, wr2_ref[...], preferred_element_type=F32))
              + br_ref[...])
    lane = lax.broadcasted_iota(I32, logits.shape, 1)
    big = jnp.int32(LANES)
    gl = jnp.where(lane < N_EXPERT_GROUPS, logits, NEG)
    gmax = jnp.max(gl, axis=-1, keepdims=True)
    gsel = jnp.min(jnp.where(gl == gmax, lane, big), axis=-1, keepdims=True)
    gsum = jnp.sum(jnp.exp(gl - gmax), axis=-1, keepdims=True)
    g_w = 1.0 / gsum
    e_lo = N_EXPERT_GROUPS + gsel * EXPERTS_PER_GROUP
    el = jnp.where((lane >= e_lo) & (lane < e_lo + EXPERTS_PER_GROUP), logits, NEG)
    m1 = jnp.max(el, axis=-1, keepdims=True)
    i1 = jnp.min(jnp.where(el == m1, lane, big), axis=-1, keepdims=True)
    el2 = jnp.where(lane == i1, NEG, el)
    m2 = jnp.max(el2, axis=-1, keepdims=True)
    i2 = jnp.min(jnp.where(el2 == m2, lane, big), axis=-1, keepdims=True)
    r = jnp.exp(m2 - m1)
    w1 = g_w / (1.0 + r)
    w2 = w1 * r
    e1 = i1 - N_EXPERT_GROUPS
    e2 = i2 - N_EXPERT_GROUPS
    oh1 = lane == e1
    oh2 = lane == e2
    oh = jnp.where(oh1 | oh2, 1.0, 0.0)
    before = jnp.dot(tri_ref[...], oh.astype(BF16), preferred_element_type=F32) + run_ref[0:1, :]
    rank1 = jnp.sum(jnp.where(oh1, before, 0.0), axis=-1, keepdims=True).astype(I32)
    rank2 = jnp.sum(jnp.where(oh2, before, 0.0), axis=-1, keepdims=True).astype(I32)
    run = run_ref[0:1, :] + jnp.sum(oh, axis=0, keepdims=True)
    run_ref[...] = jnp.broadcast_to(run, run_ref.shape)
    cnt_ref[...] = jnp.broadcast_to(run, cnt_ref.shape)
    eid_ref[...] = jnp.where(lane == 0, e1, jnp.where(lane == 1, e2, jnp.where(lane == 2, rank1,
                             jnp.where(lane == 3, rank2, 0))))
    gate_ref[...] = jnp.where(lane == 0, w1, jnp.where(lane == 1, w2, 0.0))


def _outproj(attn, ssd, xs, wa, ws, nw, wr1, wr2, br, tri):
    t = attn.shape[0]
    assert t % OUT_TM == 0
    row = lambda i: (i, 0)
    const = lambda i: (0, 0)
    once = dict(pipeline_mode=pl.Buffered(1))
    src_specs, n0 = _src_specs(xs, OUT_TM, D_MODEL, 1)
    return pl.pallas_call(
        functools.partial(_outproj_kernel, n_src=len(xs), src_tiles=n0),
        out_shape=(jax.ShapeDtypeStruct((t, D_MODEL), F32), jax.ShapeDtypeStruct((t, HALF), U32),
                   jax.ShapeDtypeStruct((t, LANES), I32), jax.ShapeDtypeStruct((t, LANES), F32),
                   jax.ShapeDtypeStruct((SUBLANES, LANES), F32)),
        grid=(t // OUT_TM,),
        in_specs=[pl.BlockSpec((OUT_TM, ATTN_WIDTH), row), pl.BlockSpec((OUT_TM, SSM_WIDTH), row)] + src_specs + [
            pl.BlockSpec((ATTN_WIDTH, D_MODEL), const, **once),
            pl.BlockSpec((SSM_WIDTH, D_MODEL), const, **once),
            pl.BlockSpec((1, D_MODEL), const),
            pl.BlockSpec((D_MODEL, 2 * LANES), const, **once),
            pl.BlockSpec((D_MODEL, LANES), const, **once),
            pl.BlockSpec((1, LANES), const),
            pl.BlockSpec((OUT_TM, OUT_TM), const, **once),
        ],
        out_specs=(pl.BlockSpec((OUT_TM, D_MODEL), row), pl.BlockSpec((OUT_TM, HALF), row),
                   pl.BlockSpec((OUT_TM, LANES), row), pl.BlockSpec((OUT_TM, LANES), row),
                   pl.BlockSpec((SUBLANES, LANES), const)),
        scratch_shapes=[pltpu.VMEM((SUBLANES, LANES), F32)],
        compiler_params=_cparams(("arbitrary",), 44),
        name="outproj",
    )(attn, ssd, *xs, wa, ws, nw, wr1, wr2, br, tri)


def _idx_copy(src_ref, idx_smem, slot, sem):
    return pltpu.make_async_copy(src_ref.at[0], idx_smem.at[pl.ds(slot, 1)], sem)


def _dispatch_kernel(zs_ref, nused_ref, idc_ref, idn_ref, h_ref, xs_hbm, idx_smem, zbuf, sem, isem, zsem):
    i = pl.program_id(0)
    n = pl.num_programs(0)
    slot = i % 2

    @pl.when(i == 0)
    def _():
        zbuf[...] = jnp.zeros_like(zbuf)

        def zero_block(row0):
            cp = pltpu.make_async_copy(zbuf, xs_hbm.at[pl.ds(row0, MOE_TM), :], zsem)
            cp.start()
            cp.wait()

        for e in range(N_EXPERTS):
            zero_block(pl.multiple_of(zs_ref[e], SUBLANES))
        n_blocks = xs_hbm.shape[0] // MOE_TM
        for b in range(n_blocks - N_EXPERTS, n_blocks):
            @pl.when(b >= nused_ref[0])
            def _():
                zero_block(b * MOE_TM)
        cp = _idx_copy(idc_ref, idx_smem, 0, isem)
        cp.start()
        cp.wait()

    @pl.when(i + 1 < n)
    def _():
        _idx_copy(idn_ref, idx_smem, 1 - slot, isem).start()

    def row(r, dst_row):
        return pltpu.make_async_copy(h_ref.at[pl.ds(r, 1), :], xs_hbm.at[pl.ds(dst_row, 1), :], sem)

    def issue(r, carry):
        row(r, idx_smem[slot, r]).start()
        row(r, idx_smem[slot, ROW_TM + r]).start()
        return carry

    def drain(r, carry):
        row(r, 0).wait()
        row(r, 0).wait()
        return carry

    lax.fori_loop(0, ROW_TM, issue, 0, unroll=8)

    @pl.when(i + 1 < n)
    def _():
        _idx_copy(idn_ref, idx_smem, 1 - slot, isem).wait()

    lax.fori_loop(0, ROW_TM, drain, 0, unroll=8)


def _dispatch(h2p, dest_idx, zstart, n_used, n_slots):
    t = h2p.shape[0]
    nt = t // ROW_TM
    grid_spec = pltpu.PrefetchScalarGridSpec(
        num_scalar_prefetch=2,
        grid=(nt,),
        in_specs=[pl.BlockSpec((1, 1, TOP_K * ROW_TM), lambda i, *_: (i, 0, 0)),
                  pl.BlockSpec((1, 1, TOP_K * ROW_TM), lambda i, *_: (jnp.minimum(i + 1, nt - 1), 0, 0)),
                  pl.BlockSpec((ROW_TM, HALF), lambda i, *_: (i, 0))],
        out_specs=pl.BlockSpec(memory_space=pl.ANY),
        scratch_shapes=[pltpu.SMEM((2, TOP_K * ROW_TM), I32), pltpu.VMEM((MOE_TM, HALF), U32),
                        pltpu.SemaphoreType.DMA, pltpu.SemaphoreType.DMA, pltpu.SemaphoreType.DMA],
    )
    return pl.pallas_call(
        _dispatch_kernel,
        out_shape=jax.ShapeDtypeStruct((n_slots, HALF), U32),
        grid_spec=grid_spec,
        compiler_params=_cparams(("arbitrary",), 16),
        name="moe_dispatch",
    )(zstart, n_used, dest_idx, dest_idx, h2p)


def _expert_kernel(be_ref, nused_ref, x_ref, wg_ref, wu_ref, wd_ref, out_ref):
    i = pl.program_id(0)

    @pl.when(i < nused_ref[0])
    def _():
        x_lo, x_hi = _unpack_pair(x_ref[...])
        x_lo = x_lo.astype(BF16)
        x_hi = x_hi.astype(BF16)
        g = (jnp.dot(x_lo, wg_ref[:HALF, :], preferred_element_type=F32)
             + jnp.dot(x_hi, wg_ref[HALF:, :], preferred_element_type=F32))
        u = (jnp.dot(x_lo, wu_ref[:HALF, :], preferred_element_type=F32)
             + jnp.dot(x_hi, wu_ref[HALF:, :], preferred_element_type=F32))
        hmid = (_silu(g) * u).astype(BF16)
        y = jnp.dot(hmid, wd_ref[...], preferred_element_type=F32)
        out_ref[...] = _pack_pair(y[:, :HALF], y[:, HALF:])

    @pl.when(i >= nused_ref[0])
    def _():
        out_ref[...] = jnp.zeros_like(out_ref)


def _experts(xs, block_expert, n_used, wg, wu, wd):
    n_slots = xs.shape[0]
    n_blocks = n_slots // MOE_TM
    blk = lambda i, be, nu: (jnp.minimum(i, nu[0] - 1), 0)
    wmap = lambda i, be, nu: (be[i], 0, 0)
    grid_spec = pltpu.PrefetchScalarGridSpec(
        num_scalar_prefetch=2,
        grid=(n_blocks,),
        in_specs=[
            pl.BlockSpec((MOE_TM, HALF), blk),
            pl.BlockSpec((None, D_MODEL, D_FF_EXPERT), wmap),
            pl.BlockSpec((None, D_MODEL, D_FF_EXPERT), wmap),
            pl.BlockSpec((None, D_FF_EXPERT, D_MODEL), wmap),
        ],
        out_specs=pl.BlockSpec((MOE_TM, HALF), lambda i, be, nu: (i, 0)),
    )
    return pl.pallas_call(
        _expert_kernel,
        out_shape=jax.ShapeDtypeStruct((n_slots, HALF), U32),
        grid_spec=grid_spec,
        compiler_params=_cparams(("arbitrary",), 48),
        name="moe_experts",
    )(block_expert, n_used, xs, wg, wu, wd)


def _combine_kernel(id0_ref, id1_ref, idn_ref, x1_ref, gate_ref, ys_hbm, nw_ref, out_ref, idx_smem, buf_a, buf_b,
                    sem, isem, *, final_norm):
    i = pl.program_id(0)
    n = pl.num_programs(0)
    slot = i % 2

    def row(tile_slot, r, k, src_row):
        buf = buf_a if k == 0 else buf_b
        return pltpu.make_async_copy(ys_hbm.at[pl.ds(src_row, 1), :], buf.at[tile_slot, pl.ds(r, 1), :],
                                     sem.at[tile_slot])

    def issue_rows(tile_slot):
        def body(r, carry):
            row(tile_slot, r, 0, idx_smem[tile_slot, r]).start()
            row(tile_slot, r, 1, idx_smem[tile_slot, ROW_TM + r]).start()
            return carry
        lax.fori_loop(0, ROW_TM, body, 0, unroll=8)

    @pl.when(i == 0)
    def _():
        cp = _idx_copy(id0_ref, idx_smem, 0, isem)
        cp.start()
        cp.wait()
        issue_rows(0)

        @pl.when(n > 1)
        def _():
            cp1 = _idx_copy(id1_ref, idx_smem, 1, isem)
            cp1.start()
            cp1.wait()

    @pl.when(i + 1 < n)
    def _():
        issue_rows(1 - slot)

    @pl.when(i + 2 < n)
    def _():
        _idx_copy(idn_ref, idx_smem, slot, isem).start()

    def drain(r, carry):
        row(slot, r, 0, 0).wait()
        row(slot, r, 1, 0).wait()
        return carry

    lax.fori_loop(0, ROW_TM, drain, 0, unroll=8)

    @pl.when(i + 2 < n)
    def _():
        _idx_copy(idn_ref, idx_smem, slot, isem).wait()

    a_lo, a_hi = _unpack_pair(buf_a[slot])
    b_lo, b_hi = _unpack_pair(buf_b[slot])
    g1 = gate_ref[:, 0:1]
    g2 = gate_ref[:, 1:2]
    o_lo = x1_ref[:, :HALF] + (a_lo * g1 + b_lo * g2)
    o_hi = x1_ref[:, HALF:] + (a_hi * g1 + b_hi * g2)
    if final_norm:
        ms = (jnp.sum(o_lo * o_lo, axis=-1, keepdims=True) + jnp.sum(o_hi * o_hi, axis=-1, keepdims=True)) / D_MODEL
        scale = lax.rsqrt(ms + EPS)
        o_lo = (o_lo * scale) * nw_ref[:, :HALF]
        o_hi = (o_hi * scale) * nw_ref[:, HALF:]
    out_ref[:, :HALF] = o_lo
    out_ref[:, HALF:] = o_hi


def _combine(x1, gate, ys, dest_idx, nw, *, tile0, n_tiles, final_norm):
    last = tile0 + n_tiles - 1
    idx_block = (1, 1, TOP_K * ROW_TM)
    return pl.pallas_call(
        functools.partial(_combine_kernel, final_norm=final_norm),
        out_shape=jax.ShapeDtypeStruct((n_tiles * ROW_TM, D_MODEL), F32),
        grid=(n_tiles,),
        in_specs=[pl.BlockSpec(idx_block, lambda i: (tile0, 0, 0)),
                  pl.BlockSpec(idx_block, lambda i: (min(tile0 + 1, last), 0, 0)),
                  pl.BlockSpec(idx_block, lambda i: (jnp.minimum(tile0 + i + 2, last), 0, 0)),
                  pl.BlockSpec((ROW_TM, D_MODEL), lambda i: (tile0 + i, 0)),
                  pl.BlockSpec((ROW_TM, LANES), lambda i: (tile0 + i, 0)),
                  pl.BlockSpec(memory_space=pl.ANY),
                  pl.BlockSpec((1, D_MODEL), lambda i: (0, 0))],
        out_specs=pl.BlockSpec((ROW_TM, D_MODEL), lambda i: (i, 0)),
        scratch_shapes=[pltpu.SMEM((2, TOP_K * ROW_TM), I32), pltpu.VMEM((2, ROW_TM, HALF), U32),
                        pltpu.VMEM((2, ROW_TM, HALF), U32), pltpu.SemaphoreType.DMA((2,)),
                        pltpu.SemaphoreType.DMA],
        compiler_params=_cparams(("arbitrary",), 24),
        name="moe_combine",
    )(dest_idx, dest_idx, dest_idx, x1, gate, ys, nw)


def _dispatch_plan(eid, rank, counts):
    t = eid.shape[0]
    n_assign = t * TOP_K
    padded = (counts + MOE_TM - 1) // MOE_TM * MOE_TM
    pend = jnp.cumsum(padded)
    pstart = pend - padded
    dest = pstart[eid] + rank
    n_blocks = -(-n_assign // MOE_TM) + N_EXPERTS
    n_slots = n_blocks * MOE_TM
    block_start = jnp.arange(n_blocks, dtype=I32) * MOE_TM
    block_expert = jnp.minimum(jnp.sum((pend[None, :] <= block_start[:, None]).astype(I32), axis=1),
                               N_EXPERTS - 1).astype(I32)
    n_used = (pend[-1] // MOE_TM).astype(I32).reshape(1)
    zstart = jnp.minimum((pstart + counts) // SUBLANES * SUBLANES, n_slots - MOE_TM).astype(I32)
    dest_idx = dest.reshape(t // ROW_TM, ROW_TM, TOP_K).transpose(0, 2, 1).reshape(t // ROW_TM, 1, TOP_K * ROW_TM)
    return dest_idx.astype(I32), block_expert, n_used, zstart, n_slots


def _seq_flags(seq_lens, tile):
    first, last = [], []
    for s in seq_lens:
        assert s % tile == 0
        n = s // tile
        first += [1] + [0] * (n - 1)
        last += [0] * (n - 1) + [1]
    return jnp.asarray(first, I32), jnp.asarray(last, I32)


def _rope_tables(seq_lens):
    pos = jnp.concatenate([jnp.arange(s, dtype=F32) for s in seq_lens])
    inv_freq = ROPE_THETA ** (-jnp.arange(0, ROPE_DIM, 2, dtype=F32) / ROPE_DIM)
    ang = pos[:, None] * inv_freq[None, :]
    d = jnp.arange(LANES) % HEAD_DIM
    cos = jnp.cos(ang)[:, d % ROPE_HALF]
    sin = jnp.sin(ang)[:, d % ROPE_HALF]
    c = jnp.where(d < ROPE_DIM, cos, 1.0)
    sa = jnp.where((d >= ROPE_HALF) & (d < ROPE_DIM), sin, 0.0)
    sb = jnp.where(d < ROPE_HALF, -sin, 0.0)
    return c, sa, sb


def _head_expand(d0):
    rows = jnp.arange(LANES)[:, None]
    cols = jnp.arange(SSM_WIDTH)[None, :] // SSM_HEAD_DIM
    return (rows == cols + d0).astype(BF16)


def _pad_lanes(v, width=LANES):
    v = v.reshape(1, -1)
    return jnp.pad(v, ((0, 0), (0, width - v.shape[1])))


def _trunk(xs, seq_lens, attn_norm_w, w_in, conv_w, conv_b, attn_sink, attn_out_norm_w, ssm_a_log, ssm_dt_bias,
           ssm_d, ssm_norm_w, w_out, ffn_norm_w, w_router_group, b_router_group, w_router_expert,
           b_router_expert, w_gate, w_up, w_down, final_norm_w, out_tiles):
    depth = w_in.shape[0]
    t = sum(seq_lens)
    rope_c, rope_sa, rope_sb = _rope_tables(seq_lens)
    tile_first, tile_last = _seq_flags(seq_lens, ATTN_TQ)
    chunk_first, chunk_last = _seq_flags(seq_lens, CHUNK)
    e_fwd, e_bwd = _head_expand(0), _head_expand(SSM_HEADS)
    shift = _conv_shift_matrix()
    tri = (jnp.arange(OUT_TM)[:, None] > jnp.arange(OUT_TM)[None, :]).astype(BF16)
    z_end = ATTN_WIDTH + 2 * KV_WIDTH + SSM_WIDTH
    outs = None
    for l in range(depth):
        w = w_in[l]
        w_main = jnp.concatenate([w[:, :ATTN_WIDTH + 2 * KV_WIDTH], w[:, z_end:z_end + CONV_CH],
                                  w[:, ATTN_WIDTH + 2 * KV_WIDTH:z_end]], axis=1).astype(BF16)
        w_dt = jnp.pad(w[:, z_end + CONV_CH:], ((0, 0), (0, LANES - 2 * SSM_HEADS))).astype(BF16)
        proj, dt_raw = _proj(xs, attn_norm_w[l].reshape(1, -1), w_main, w_dt, rope_c, rope_sa, rope_sb)
        attn = _attention(proj, tile_first, tile_last, attn_sink[l] * LOG2E, attn_out_norm_w[l].reshape(1, -1))
        cw = jnp.pad(conv_w[l], ((0, SUBLANES - CONV_WIDTH), (0, 0)))
        cb = conv_b[l].reshape(1, -1)
        dtb = _pad_lanes(ssm_dt_bias[l])
        alog = _pad_lanes(ssm_a_log[l])
        dexp = jnp.repeat(ssm_d[l], SSM_HEAD_DIM).reshape(1, -1)
        y_fwd = _ssd(proj, dt_raw, chunk_first, chunk_last, shift, cw, cb, dtb, alog, e_fwd, dexp, backward=False)
        ssd = _ssd(proj, dt_raw, chunk_first, chunk_last, shift, cw, cb, dtb, alog, e_bwd,
                   ssm_norm_w[l].reshape(1, -1), backward=True, y_fwd=y_fwd)
        wo = w_out[l].astype(BF16)
        wr = jnp.pad(jnp.concatenate([w_router_group[l], w_router_expert[l]], axis=1),
                     ((0, 0), (0, LANES - N_EXPERT_GROUPS - N_EXPERTS)))
        wr_hi, wr_lo = _split2(wr)
        br = _pad_lanes(jnp.concatenate([b_router_group[l], b_router_expert[l]]))
        x1, h2p, eid, gate, cnt = _outproj(attn, ssd, xs, wo[:ATTN_WIDTH], wo[ATTN_WIDTH:],
                                           ffn_norm_w[l].reshape(1, -1), jnp.concatenate([wr_hi, wr_lo], axis=1),
                                           wr_hi, br, tri)
        counts = cnt[0, :N_EXPERTS].astype(I32)
        dest_idx, block_expert, n_used, zstart, n_slots = _dispatch_plan(eid[:, :TOP_K], eid[:, TOP_K:2 * TOP_K],
                                                                        counts)
        xslots = _dispatch(h2p, dest_idx, zstart, n_used, n_slots)
        ys = _experts(xslots, block_expert, n_used, w_gate[l].astype(BF16), w_up[l].astype(BF16),
                      w_down[l].astype(BF16))
        nw = final_norm_w.reshape(1, -1)
        if l == depth - 1:
            outs, tile0 = [], 0
            for nt in out_tiles:
                outs.append(_combine(x1, gate, ys, dest_idx, nw, tile0=tile0, n_tiles=nt, final_norm=True))
                tile0 += nt
        else:
            xs = [_combine(x1, gate, ys, dest_idx, nw, tile0=0, n_tiles=t // ROW_TM, final_norm=False)]
    return outs


def kernel(x_prompt, x_sample, attn_norm_w, w_in, conv_w, conv_b, attn_sink, attn_out_norm_w, ssm_a_log,
           ssm_dt_bias, ssm_d, ssm_norm_w, w_out, ffn_norm_w, w_router_group, b_router_group, w_router_expert,
           b_router_expert, w_gate, w_up, w_down, final_norm_w):
    bp, sp, d = x_prompt.shape
    bs, ss, _ = x_sample.shape
    seq_lens = [sp] * bp + [ss] * bs
    xs = [x_prompt.reshape(bp * sp, d), x_sample.reshape(bs * ss, d)]
    y_prompt, y_sample = _trunk(xs, seq_lens, attn_norm_w, w_in, conv_w, conv_b, attn_sink, attn_out_norm_w,
                                ssm_a_log, ssm_dt_bias, ssm_d, ssm_norm_w, w_out, ffn_norm_w, w_router_group,
                                b_router_group, w_router_expert, b_router_expert, w_gate, w_up, w_down,
                                final_norm_w, out_tiles=[bp * sp // ROW_TM, bs * ss // ROW_TM])
    return (y_prompt.reshape(bp, sp, d), y_sample.reshape(bs, ss, d))
```

```python
import functools
import math

import numpy as np
import jax
import jax.numpy as jnp
from jax import lax
from jax.experimental import pallas as pl
from jax.experimental.pallas import tpu as pltpu

D_MODEL = 2048
N_Q_HEADS = 16
N_KV_HEADS = 4
HEAD_DIM = 64
Q_GROUP = N_Q_HEADS // N_KV_HEADS
ATTN_WIDTH = N_Q_HEADS * HEAD_DIM
KV_WIDTH = N_KV_HEADS * HEAD_DIM
WINDOW = 128
BLOCK = 128
ROPE_THETA = 500000.0
ROPE_DIM = HEAD_DIM // 4
ROPE_HALF = ROPE_DIM // 2
SSM_HEADS = 16
SSM_HEAD_DIM = 64
SSM_WIDTH = SSM_HEADS * SSM_HEAD_DIM
SSM_GROUPS = 2
HEADS_PER_GROUP = SSM_HEADS // SSM_GROUPS
D_STATE = 128
BC_WIDTH = SSM_GROUPS * D_STATE
CONV_WIDTH = 5
CONV_PAD = CONV_WIDTH // 2
CONV_CH = SSM_WIDTH + 2 * BC_WIDTH
CHUNK = 128
N_EXPERT_GROUPS = 4
EXPERTS_PER_GROUP = 8
N_EXPERTS = N_EXPERT_GROUPS * EXPERTS_PER_GROUP
TOP_K = 2
D_FF_EXPERT = 1024
EPS = 1e-6

LANES = 128
SUBLANES = 8
BF16_SUBLANES = 16
MIB = 1024 * 1024

COL_Q = 0
COL_K = ATTN_WIDTH
COL_V = COL_K + KV_WIDTH
COL_XBC = COL_V + KV_WIDTH
COL_Z = COL_XBC + CONV_CH
PROJ_MAIN = COL_Z + SSM_WIDTH
PROJ_TM = 512
PROJ_TN = 1024
ATTN_TQ = 512
OUT_TM = 256
ROUTE_TM = 512
SSD_NCH = 4
MOE_TM = 256
ROW_TM = 256
HALF = D_MODEL // 2
CONV_K = 2 * CHUNK
NEG = -1e30
LOG2E = math.log2(math.e)
Q_SCALE = HEAD_DIM ** -0.5 * LOG2E
F32 = jnp.float32
BF16 = jnp.bfloat16
U32 = jnp.uint32
I32 = jnp.int32


def _cparams(semantics, vmem_mib):
    return pltpu.CompilerParams(dimension_semantics=semantics, vmem_limit_bytes=vmem_mib * MIB)


def _rms(x, w):
    ms = jnp.mean(x * x, axis=-1, keepdims=True)
    return (x * lax.rsqrt(ms + EPS)) * w


def _silu(x):
    return x / (1.0 + jnp.exp(-x))


def _split2(v):
    hi = v.astype(BF16)
    lo = (v - hi.astype(F32)).astype(BF16)
    return hi, lo


def _split3(v):
    hi = v.astype(BF16)
    r = v - hi.astype(F32)
    mid = r.astype(BF16)
    lo = (r - mid.astype(F32)).astype(BF16)
    return hi, mid, lo


def _pack_pair(lo, hi):
    lo_b = lax.bitcast_convert_type(lo.astype(BF16).astype(F32), U32)
    hi_b = lax.bitcast_convert_type(hi.astype(BF16).astype(F32), U32)
    return (lo_b >> 16) | (hi_b & jnp.uint32(0xFFFF0000))


ROW_TILE = (SUBLANES, HALF // SUBLANES)


def _rows_to_tiles(w):
    return w.reshape((w.shape[0],) + ROW_TILE)


def _tiles_to_rows(w):
    return w.reshape(w.shape[0], HALF)


def _unpack_pair(w):
    lo = lax.bitcast_convert_type(w << 16, F32)
    hi = lax.bitcast_convert_type(w & jnp.uint32(0xFFFF0000), F32)
    return lo, hi


def _select_src(i, x_refs, src_tiles):
    if len(x_refs) == 1:
        return x_refs[0][...]
    return jnp.where(i < src_tiles, x_refs[0][...], x_refs[1][...])


def _src_specs(xs, tm, width, nargs):
    if len(xs) == 1:
        return [pl.BlockSpec((tm, width), lambda i, *_: (i, 0))], 0
    n0 = xs[0].shape[0] // tm
    return [pl.BlockSpec((tm, width), lambda i, *_: (jnp.minimum(i, n0 - 1), 0)),
            pl.BlockSpec((tm, width), lambda i, *_: (jnp.maximum(i - n0, 0), 0))], n0


def _proj_kernel(*refs, n_src, src_tiles):
    x_refs = refs[:n_src]
    nw_ref, w_ref, wdt_ref, c_ref, sa_ref, sb_ref, out_ref, dt_ref = refs[n_src:]
    i = pl.program_id(0)
    h = _rms(_select_src(i, x_refs, src_tiles), nw_ref[...]).astype(BF16)
    dt_ref[...] = jnp.dot(h, wdt_ref[...], preferred_element_type=F32)

    def rope(a):
        return (a * c_ref[...] + pltpu.roll(a, ROPE_HALF, 1) * sa_ref[...]
                + pltpu.roll(a, LANES - ROPE_HALF, 1) * sb_ref[...])

    for jt in range(PROJ_MAIN // PROJ_TN):
        c0 = jt * PROJ_TN
        acc = jnp.dot(h, w_ref[:, c0:c0 + PROJ_TN], preferred_element_type=F32)
        for cb in range(PROJ_TN // LANES):
            col = c0 + cb * LANES
            a = acc[:, cb * LANES:(cb + 1) * LANES]
            if col < COL_V:
                a = rope(a)
            if col < COL_K:
                a = a * Q_SCALE
            out_ref[:, col:col + LANES] = a.astype(BF16)


def _proj(xs, nw, w_main, w_dt, rope_c, rope_sa, rope_sb):
    t = sum(x.shape[0] for x in xs)
    assert t % PROJ_TM == 0 and all(x.shape[0] % PROJ_TM == 0 for x in xs)
    row = lambda i: (i, 0)
    const = lambda i: (0, 0)
    once = dict(pipeline_mode=pl.Buffered(1))
    src_specs, n0 = _src_specs(xs, PROJ_TM, D_MODEL, 1)
    return pl.pallas_call(
        functools.partial(_proj_kernel, n_src=len(xs), src_tiles=n0),
        out_shape=(jax.ShapeDtypeStruct((t, PROJ_MAIN), BF16), jax.ShapeDtypeStruct((t, LANES), F32)),
        grid=(t // PROJ_TM,),
        in_specs=src_specs + [
            pl.BlockSpec((1, D_MODEL), const),
            pl.BlockSpec((D_MODEL, PROJ_MAIN), const, **once),
            pl.BlockSpec((D_MODEL, LANES), const, **once),
            pl.BlockSpec((PROJ_TM, LANES), row),
            pl.BlockSpec((PROJ_TM, LANES), row),
            pl.BlockSpec((PROJ_TM, LANES), row),
        ],
        out_specs=(pl.BlockSpec((PROJ_TM, PROJ_MAIN), row), pl.BlockSpec((PROJ_TM, LANES), row)),
        compiler_params=_cparams(("arbitrary",), 56),
        name="proj",
    )(*xs, nw, w_main, w_dt, rope_c, rope_sa, rope_sb)


def _attn_kernel(tf_ref, tl_ref, sink_ref, q_ref, kc_ref, vc_ref, kp_ref, vp_ref, kn_ref, vn_ref,
                 nw_ref, o_ref):
    i = pl.program_id(0)
    kw = jnp.concatenate([kp_ref[...], kc_ref[...], kn_ref[...]], axis=0)
    vw = jnp.concatenate([vp_ref[...], vc_ref[...], vn_ref[...]], axis=0)
    rows = kw.shape[0]
    lane_w = lax.broadcasted_iota(I32, (rows, LANES), 1)
    low = lane_w < HEAD_DIM
    def pad_pair(slab, g):
        s = slab.astype(F32)
        if g % 2 == 0:
            e = jnp.where(low, s, 0.0)
            o = pltpu.roll(e, HEAD_DIM, 1)
        else:
            o = jnp.where(low, 0.0, s)
            e = pltpu.roll(o, HEAD_DIM, 1)
        return e, o

    ke, ko, vx = [], [], []
    for g in range(N_KV_HEADS):
        c0 = (g // 2) * LANES
        e, o = pad_pair(kw[:, c0:c0 + LANES], g)
        ke.append(e.astype(BF16))
        ko.append(o.astype(BF16))
        e, _ = pad_pair(vw[:, c0:c0 + LANES], g)
        vx.append(jnp.where(lane_w == HEAD_DIM, 1.0, e).astype(BF16))

    qi = lax.broadcasted_iota(I32, (BLOCK, BLOCK), 0)
    kk = lax.broadcasted_iota(I32, (BLOCK, BLOCK), 1)
    bias_prev = jnp.where(kk >= qi, 0.0, NEG)
    bias_next = jnp.where(kk <= qi, 0.0, NEG)
    lane_b = lax.broadcasted_iota(I32, (BLOCK, LANES), 1)
    nsb = ATTN_TQ // BLOCK
    for sb in range(nsb):
        bp, bn = bias_prev, bias_next
        if sb == 0:
            bp = jnp.where(tf_ref[i] == 1, NEG, bias_prev)
        if sb == nsb - 1:
            bn = jnp.where(tl_ref[i] == 1, NEG, bias_next)
        r0 = sb * BLOCK
        slabs = []
        for g in range(N_KV_HEADS):
            q0 = g * Q_GROUP * HEAD_DIM
            lhs = jnp.concatenate([q_ref[r0:r0 + BLOCK, q0:q0 + LANES],
                                   q_ref[r0:r0 + BLOCK, q0 + LANES:q0 + 2 * LANES]], axis=0)
            rhs = jnp.concatenate([ke[g][r0:r0 + 3 * BLOCK], ko[g][r0:r0 + 3 * BLOCK]], axis=0)
            s = lax.dot_general(lhs, rhs, (((1,), (1,)), ((), ())), preferred_element_type=F32)
            ps, ms = [], []
            for hi in range(Q_GROUP):
                c0 = (hi % 2) * 3 * BLOCK
                sq = s[(hi // 2) * BLOCK:(hi // 2 + 1) * BLOCK, c0:c0 + 3 * BLOCK]
                s0 = sq[:, :BLOCK] + bp
                s1 = sq[:, BLOCK:2 * BLOCK]
                s2 = sq[:, 2 * BLOCK:] + bn
                m = jnp.max(jnp.maximum(jnp.maximum(s0, s1), s2), axis=-1, keepdims=True)
                m = jnp.maximum(m, sink_ref[g * Q_GROUP + hi])
                ps.append(jnp.concatenate([jnp.exp2(s0 - m).astype(BF16), jnp.exp2(s1 - m).astype(BF16),
                                           jnp.exp2(s2 - m).astype(BF16)], axis=1))
                ms.append(m)
            o4 = jnp.dot(jnp.concatenate(ps, axis=0), vx[g][r0:r0 + 3 * BLOCK], preferred_element_type=F32)
            normed = []
            for hi in range(Q_GROUP):
                oh = o4[hi * BLOCK:(hi + 1) * BLOCK]
                den = oh[:, HEAD_DIM:HEAD_DIM + 1] + jnp.exp2(sink_ref[g * Q_GROUP + hi] - ms[hi])
                normed.append(oh * (1.0 / den))
            for half in range(2):
                slabs.append(jnp.where(lane_b < HEAD_DIM, normed[2 * half],
                                       pltpu.roll(normed[2 * half + 1], HEAD_DIM, 1)))
        o = jnp.concatenate(slabs, axis=1)
        o_ref[r0:r0 + BLOCK, :] = _rms(o, nw_ref[...]).astype(BF16)


def _attention(proj, tile_first, tile_last, sink2, nw):
    t = proj.shape[0]
    assert t % ATTN_TQ == 0
    nb = t // BLOCK
    r = ATTN_TQ // BLOCK
    kcol, vcol = COL_K // KV_WIDTH, COL_V // KV_WIDTH
    prev = lambda i: jnp.maximum(i * r - 1, 0)
    nxt = lambda i: jnp.minimum(i * r + r, nb - 1)
    grid_spec = pltpu.PrefetchScalarGridSpec(
        num_scalar_prefetch=3,
        grid=(t // ATTN_TQ,),
        in_specs=[
            pl.BlockSpec((ATTN_TQ, ATTN_WIDTH), lambda i, *_: (i, COL_Q // ATTN_WIDTH)),
            pl.BlockSpec((ATTN_TQ, KV_WIDTH), lambda i, *_: (i, kcol)),
            pl.BlockSpec((ATTN_TQ, KV_WIDTH), lambda i, *_: (i, vcol)),
            pl.BlockSpec((BLOCK, KV_WIDTH), lambda i, *_: (prev(i), kcol)),
            pl.BlockSpec((BLOCK, KV_WIDTH), lambda i, *_: (prev(i), vcol)),
            pl.BlockSpec((BLOCK, KV_WIDTH), lambda i, *_: (nxt(i), kcol)),
            pl.BlockSpec((BLOCK, KV_WIDTH), lambda i, *_: (nxt(i), vcol)),
            pl.BlockSpec((1, ATTN_WIDTH), lambda i, *_: (0, 0)),
        ],
        out_specs=pl.BlockSpec((ATTN_TQ, ATTN_WIDTH), lambda i, *_: (i, 0)),
    )
    return pl.pallas_call(
        _attn_kernel,
        out_shape=jax.ShapeDtypeStruct((t, ATTN_WIDTH), BF16),
        grid_spec=grid_spec,
        compiler_params=_cparams(("arbitrary",), 32),
        name="attn",
    )(tile_first, tile_last, sink2, proj, proj, proj, proj, proj, proj, proj, nw)


def _ssd_kernel(first_ref, last_ref, *refs, backward):
    if backward:
        xc_ref, dt_ref, *rest = refs
        xp_ref = xn_ref = None
    else:
        xc_ref, xp_ref, xn_ref, dt_ref, *rest = refs
    h_ref = rest[-1]
    i = pl.program_id(0)
    tile = (pl.num_programs(0) - 1 - i) if backward else i
    order = range(SSD_NCH - 1, -1, -1) if backward else range(SSD_NCH)

    @pl.when(i == 0)
    def _():
        h_ref[...] = jnp.zeros_like(h_ref)

    h = h_ref[...]
    for k in order:
        h = _ssd_chunk(k, tile * SSD_NCH + k, h, first_ref, last_ref, xc_ref, xp_ref, xn_ref, dt_ref, rest,
                       backward)
    h_ref[...] = h


def _ssd_chunk(k, c, hprev, first_ref, last_ref, xc_ref, xp_ref, xn_ref, dt_ref, rest, backward):
    is_first = first_ref[c] == 1
    is_last = last_ref[c] == 1
    hprev = jnp.where(is_last if backward else is_first, jnp.zeros_like(hprev), hprev)
    r0 = k * CHUNK
    rows = pl.ds(r0, CHUNK)

    if backward:
        (z_ref, yf_ref, dtb_ref, alog_ref, e_ref, nw_ref, out_ref, _) = rest
        u = xc_ref[rows, :].astype(F32)
    else:
        (sh_ref, cw_ref, cb_ref, dtb_ref, alog_ref, e_ref, dexp_ref, out_ref, u_ref, _) = rest
        halo = BF16_SUBLANES
        xc = xc_ref[rows, :]
        no_rows = jnp.zeros((halo, CONV_CH), BF16)
        before = xp_ref[...] if k == 0 else xc_ref[r0 - halo:r0, :]
        after = xn_ref[...] if k == SSD_NCH - 1 else xc_ref[r0 + CHUNK:r0 + CHUNK + halo, :]
        xp = jnp.where(is_first, no_rows, before)
        xn = jnp.where(is_last, no_rows, after)
        xw = jnp.concatenate([xc, xp, xn, jnp.zeros((CONV_K - CHUNK - 2 * halo, CONV_CH), BF16)], axis=0)
        shifted = jnp.dot(sh_ref[...], xw, preferred_element_type=F32)
        acc = cb_ref[...] + xc.astype(F32) * cw_ref[CONV_PAD:CONV_PAD + 1, :]
        taps = [tap for tap in range(CONV_WIDTH) if tap != CONV_PAD]
        for n, tap in enumerate(taps):
            acc = acc + shifted[n * CHUNK:(n + 1) * CHUNK] * cw_ref[tap:tap + 1, :]
        u = _silu(acc)
        u_ref[rows, :] = u.astype(BF16)
    xs = u[:, :SSM_WIDTH]
    bm = u[:, SSM_WIDTH:SSM_WIDTH + BC_WIDTH]
    cm = u[:, SSM_WIDTH + BC_WIDTH:]

    d0 = SSM_HEADS if backward else 0
    draw = dt_ref[rows, :] + dtb_ref[...]
    dt = jnp.maximum(draw, 0.0) + jnp.log1p(jnp.exp(-jnp.abs(draw)))
    a = dt * (-jnp.exp(alog_ref[...]))
    li = lax.broadcasted_iota(I32, (CHUNK, CHUNK), 0)
    ti = lax.broadcasted_iota(I32, (CHUNK, CHUNK), 1)
    causal = (li <= ti) if backward else (li >= ti)
    tri = jnp.where(causal, 1.0, 0.0).astype(BF16)
    a_hi, a_mid, a_lo = _split3(a)
    acum = (jnp.dot(tri, a_hi, preferred_element_type=F32) + jnp.dot(tri, a_mid, preferred_element_type=F32)
            + jnp.dot(tri, a_lo, preferred_element_type=F32))
    end = 0 if backward else CHUNK - 1
    tot = acum[end:end + 1, :]
    exp_a = jnp.exp(acum)
    wdec = dt * jnp.exp(tot - acum)
    stack = jnp.concatenate(_split2(exp_a) + (wdec.astype(BF16),), axis=0)
    ex = jnp.dot(stack, e_ref[...], preferred_element_type=F32)
    ea_e = ex[0:CHUNK] + ex[CHUNK:2 * CHUNK]
    wd_e = ex[2 * CHUNK:3 * CHUNK]
    xd = (xs * wd_e).astype(BF16)
    acum_t = (acum - jnp.log(dt)).T
    hb = hprev.astype(BF16)
    lane = lax.broadcasted_iota(I32, (CHUNK, LANES), 1)
    gw = HEADS_PER_GROUP * SSM_HEAD_DIM
    ys = []
    sts = []
    for g in range(SSM_GROUPS):
        bg = bm[:, g * D_STATE:(g + 1) * D_STATE]
        cg = cm[:, g * D_STATE:(g + 1) * D_STATE].astype(BF16)
        cb = lax.dot_general(cg, bg.astype(BF16), (((1,), (1,)), ((), ())), preferred_element_type=F32)
        yoff = jnp.dot(cg, hb[:, g * gw:(g + 1) * gw], preferred_element_type=F32)
        sts.append(jnp.dot(bg.T.astype(BF16), xd[:, g * gw:(g + 1) * gw], preferred_element_type=F32))
        for pr in range(HEADS_PER_GROUP // 2):
            h0 = g * HEADS_PER_GROUP + 2 * pr
            ms = []
            for hh in (h0, h0 + 1):
                col = acum[:, d0 + hh:d0 + hh + 1]
                row = acum_t[d0 + hh:d0 + hh + 1, :]
                decay = jnp.exp(jnp.where(causal, col - row, NEG))
                ms.append((cb * decay).astype(BF16))
            lhs = jnp.concatenate(ms, axis=1)
            c0 = (h0 // 2) * LANES
            slab = xs[:, c0:c0 + LANES]
            rhs = jnp.concatenate([jnp.where(lane < SSM_HEAD_DIM, slab, 0.0),
                                   jnp.where(lane >= SSM_HEAD_DIM, slab, 0.0)], axis=0).astype(BF16)
            yd = jnp.dot(lhs, rhs, preferred_element_type=F32)
            ys.append(yd + yoff[:, pr * LANES:(pr + 1) * LANES] * ea_e[:, c0:c0 + LANES])
    y = jnp.concatenate(ys, axis=1)
    if backward:
        yt = yf_ref[rows, :] + y
        gz = yt * _silu(z_ref[rows, :].astype(F32))
        out_ref[rows, :] = _rms(gz, nw_ref[...]).astype(BF16)
    else:
        out_ref[rows, :] = y + xs * dexp_ref[...]
    return hprev * ea_e[end:end + 1, :] + jnp.concatenate(sts, axis=1)


def _conv_shift_matrix():
    halo = BF16_SUBLANES
    taps = [k for k in range(CONV_WIDTH) if k != CONV_PAD]
    m = np.zeros((len(taps) * CHUNK, CONV_K), np.float32)
    for n, k in enumerate(taps):
        for t in range(CHUNK):
            j = t + k - CONV_PAD
            if j < 0:
                col = CHUNK + halo + j
            elif j >= CHUNK:
                col = CHUNK + halo + (j - CHUNK)
            else:
                col = j
            m[n * CHUNK + t, col] = 1.0
    return jnp.asarray(m, BF16)


def _ssd(proj, dt_raw, chunk_first, chunk_last, dt_bias, a_log, expand, extra, *, backward, conv=None,
         u_fwd=None, y_fwd=None):
    t = proj.shape[0]
    tm = SSD_NCH * CHUNK
    assert t % tm == 0
    nc = t // tm
    hb = tm // BF16_SUBLANES
    n_halo = t // BF16_SUBLANES
    xcol = COL_XBC // CONV_CH
    ch = (lambda i: nc - 1 - i) if backward else (lambda i: i)
    const = lambda i, *_: (0, 0)
    tile = lambda width, col=0: pl.BlockSpec((tm, width), lambda i, *_: (ch(i), col))
    small = [pl.BlockSpec((1, LANES), const), pl.BlockSpec((1, LANES), const),
             pl.BlockSpec((LANES, SSM_WIDTH), const), pl.BlockSpec((1, SSM_WIDTH), const)]
    if backward:
        in_specs = [tile(CONV_CH), tile(LANES), tile(SSM_WIDTH, COL_Z // SSM_WIDTH), tile(SSM_WIDTH)] + small
        args = [u_fwd, dt_raw, proj, y_fwd, dt_bias, a_log, expand, extra]
        out_shape = jax.ShapeDtypeStruct((t, SSM_WIDTH), BF16)
        out_specs = tile(SSM_WIDTH)
    else:
        shift, conv_w, conv_b = conv
        in_specs = [
            tile(CONV_CH, xcol),
            pl.BlockSpec((BF16_SUBLANES, CONV_CH), lambda i, *_: (jnp.maximum(ch(i) * hb - 1, 0), xcol)),
            pl.BlockSpec((BF16_SUBLANES, CONV_CH), lambda i, *_: (jnp.minimum(ch(i) * hb + hb, n_halo - 1), xcol)),
            tile(LANES),
            pl.BlockSpec(shift.shape, const),
            pl.BlockSpec((SUBLANES, CONV_CH), const),
            pl.BlockSpec((1, CONV_CH), const),
        ] + small
        args = [proj, proj, proj, dt_raw, shift, conv_w, conv_b, dt_bias, a_log, expand, extra]
        out_shape = (jax.ShapeDtypeStruct((t, SSM_WIDTH), F32), jax.ShapeDtypeStruct((t, CONV_CH), BF16))
        out_specs = (tile(SSM_WIDTH), tile(CONV_CH))
    grid_spec = pltpu.PrefetchScalarGridSpec(
        num_scalar_prefetch=2,
        grid=(nc,),
        in_specs=in_specs,
        out_specs=out_specs,
        scratch_shapes=[pltpu.VMEM((D_STATE, SSM_WIDTH), F32)],
    )
    return pl.pallas_call(
        functools.partial(_ssd_kernel, backward=backward),
        out_shape=out_shape,
        grid_spec=grid_spec,
        compiler_params=_cparams(("arbitrary",), 32),
        name="ssd_bwd" if backward else "ssd_fwd",
    )(chunk_first, chunk_last, *args)


def _outproj_kernel(*refs, n_src, src_tiles):
    a_ref, s_ref = refs[:2]
    x_refs = refs[2:2 + n_src]
    wa_ref, ws_ref, nw_ref, wr1_ref, wr2_ref, br_ref, x1_ref, h2_ref, lg_ref = refs[2 + n_src:]
    i = pl.program_id(0)
    y = (jnp.dot(a_ref[...], wa_ref[...], preferred_element_type=F32)
         + jnp.dot(s_ref[...], ws_ref[...], preferred_element_type=F32))
    x1 = _select_src(i, x_refs, src_tiles) + y
    x1_ref[...] = x1
    h2 = _rms(x1, nw_ref[...])
    h2_ref[...] = _rows_to_tiles(_pack_pair(h2[:, :HALF], h2[:, HALF:]))
    h_hi, h_lo = _split2(h2)
    part = jnp.dot(h_hi, wr1_ref[...], preferred_element_type=F32)
    lg_ref[...] = (part[:, :LANES] + (part[:, LANES:] + jnp.dot(h_lo, wr2_ref[...], preferred_element_type=F32))
                   + br_ref[...])


def _outproj(attn, ssd, xs, w_out, layer, nw, wr1, wr2, br):
    t = attn.shape[0]
    assert t % OUT_TM == 0
    row = lambda i: (i, 0)
    const = lambda i: (0, 0)
    once = dict(pipeline_mode=pl.Buffered(1))
    src_specs, n0 = _src_specs(xs, OUT_TM, D_MODEL, 1)
    return pl.pallas_call(
        functools.partial(_outproj_kernel, n_src=len(xs), src_tiles=n0),
        out_shape=(jax.ShapeDtypeStruct((t, D_MODEL), F32), jax.ShapeDtypeStruct((t,) + ROW_TILE, U32),
                   jax.ShapeDtypeStruct((t, LANES), F32)),
        grid=(t // OUT_TM,),
        in_specs=[pl.BlockSpec((OUT_TM, ATTN_WIDTH), row), pl.BlockSpec((OUT_TM, SSM_WIDTH), row)] + src_specs + [
            pl.BlockSpec((None, ATTN_WIDTH, D_MODEL), lambda i: (layer, 0, 0), **once),
            pl.BlockSpec((None, SSM_WIDTH, D_MODEL), lambda i: (layer, ATTN_WIDTH // SSM_WIDTH, 0), **once),
            pl.BlockSpec((1, D_MODEL), const),
            pl.BlockSpec((D_MODEL, 2 * LANES), const, **once),
            pl.BlockSpec((D_MODEL, LANES), const, **once),
            pl.BlockSpec((1, LANES), const),
        ],
        out_specs=(pl.BlockSpec((OUT_TM, D_MODEL), row), pl.BlockSpec((OUT_TM,) + ROW_TILE, lambda i: (i, 0, 0)),
                   pl.BlockSpec((OUT_TM, LANES), row)),
        compiler_params=_cparams(("arbitrary",), 44),
        name="outproj",
    )(attn, ssd, *xs, w_out, w_out, nw, wr1, wr2, br)


def _route_kernel(lg_ref, tri_ref, eid_ref, gate_ref, cnt_ref, run_ref):
    i = pl.program_id(0)

    @pl.when(i == 0)
    def _():
        run_ref[...] = jnp.zeros_like(run_ref)

    logits = lg_ref[...]
    lane = lax.broadcasted_iota(I32, logits.shape, 1)
    big = jnp.int32(LANES)
    gl = jnp.where(lane < N_EXPERT_GROUPS, logits, NEG)
    gmax = jnp.max(gl, axis=-1, keepdims=True)
    gsel = jnp.min(jnp.where(gl == gmax, lane, big), axis=-1, keepdims=True)
    gsum = jnp.sum(jnp.exp(gl - gmax), axis=-1, keepdims=True)
    g_w = 1.0 / gsum
    e_lo = N_EXPERT_GROUPS + gsel * EXPERTS_PER_GROUP
    el = jnp.where((lane >= e_lo) & (lane < e_lo + EXPERTS_PER_GROUP), logits, NEG)
    m1 = jnp.max(el, axis=-1, keepdims=True)
    i1 = jnp.min(jnp.where(el == m1, lane, big), axis=-1, keepdims=True)
    el2 = jnp.where(lane == i1, NEG, el)
    m2 = jnp.max(el2, axis=-1, keepdims=True)
    i2 = jnp.min(jnp.where(el2 == m2, lane, big), axis=-1, keepdims=True)
    r = jnp.exp(m2 - m1)
    w1 = g_w / (1.0 + r)
    w2 = w1 * r
    e1 = i1 - N_EXPERT_GROUPS
    e2 = i2 - N_EXPERT_GROUPS
    oh1 = lane == e1
    oh2 = lane == e2
    oh = jnp.where(oh1 | oh2, 1.0, 0.0)
    before = jnp.dot(tri_ref[...], oh.astype(BF16), preferred_element_type=F32) + run_ref[0:1, :]
    rank1 = jnp.sum(jnp.where(oh1, before, 0.0), axis=-1, keepdims=True).astype(I32)
    rank2 = jnp.sum(jnp.where(oh2, before, 0.0), axis=-1, keepdims=True).astype(I32)
    run = run_ref[0:1, :] + jnp.sum(oh, axis=0, keepdims=True)
    run_ref[...] = jnp.broadcast_to(run, run_ref.shape)
    cnt_ref[...] = jnp.broadcast_to(run, cnt_ref.shape)
    eid_ref[...] = jnp.where(lane == 0, e1, jnp.where(lane == 1, e2, jnp.where(lane == 2, rank1,
                             jnp.where(lane == 3, rank2, 0))))
    gate_ref[...] = jnp.where(lane == 0, w1, jnp.where(lane == 1, w2, 0.0))


def _route(logits, tri):
    t = logits.shape[0]
    tm = tri.shape[0]
    assert t % tm == 0
    row = lambda i: (i, 0)
    const = lambda i: (0, 0)
    return pl.pallas_call(
        _route_kernel,
        out_shape=(jax.ShapeDtypeStruct((t, LANES), I32), jax.ShapeDtypeStruct((t, LANES), F32),
                   jax.ShapeDtypeStruct((SUBLANES, LANES), F32)),
        grid=(t // tm,),
        in_specs=[pl.BlockSpec((tm, LANES), row),
                  pl.BlockSpec((tm, tm), const, pipeline_mode=pl.Buffered(1))],
        out_specs=(pl.BlockSpec((tm, LANES), row), pl.BlockSpec((tm, LANES), row),
                   pl.BlockSpec((SUBLANES, LANES), const)),
        scratch_shapes=[pltpu.VMEM((SUBLANES, LANES), F32)],
        compiler_params=_cparams(("arbitrary",), 24),
        name="route",
    )(logits, tri)


def _idx_copy(src_ref, idx_smem, slot, sem):
    return pltpu.make_async_copy(src_ref.at[0], idx_smem.at[pl.ds(slot, 1)], sem)


def _dispatch_kernel(zs_ref, nused_ref, idc_ref, idn_ref, h_hbm, xs_hbm, idx_smem, zbuf, sem, isem, zsem):
    i = pl.program_id(0)
    n = pl.num_programs(0)
    slot = i % 2

    @pl.when(i == 0)
    def _():
        zbuf[...] = jnp.zeros_like(zbuf)

        def zero_block(row0):
            cp = pltpu.make_async_copy(zbuf, xs_hbm.at[pl.ds(row0, MOE_TM)], zsem)
            cp.start()
            cp.wait()

        for e in range(N_EXPERTS):
            zero_block(zs_ref[e])
        n_blocks = xs_hbm.shape[0] // MOE_TM
        for b in range(n_blocks - N_EXPERTS, n_blocks):
            @pl.when(b >= nused_ref[0])
            def _():
                zero_block(b * MOE_TM)
        cp = _idx_copy(idc_ref, idx_smem, 0, isem)
        cp.start()
        cp.wait()

    @pl.when(i + 1 < n)
    def _():
        _idx_copy(idn_ref, idx_smem, 1 - slot, isem).start()

    def row(src_row, dst_row, s):
        return pltpu.make_async_copy(h_hbm.at[pl.ds(src_row, 1)], xs_hbm.at[pl.ds(dst_row, 1)], sem.at[s])

    def issue(r, carry):
        row(i * ROW_TM + r, idx_smem[slot, r], slot).start(priority=0)
        row(i * ROW_TM + r, idx_smem[slot, ROW_TM + r], slot).start(priority=1)
        return carry

    def drain(s):
        def body(r, carry):
            row(0, 0, s).wait()
            row(0, 0, s).wait()
            return carry
        lax.fori_loop(0, ROW_TM, body, 0, unroll=8)

    lax.fori_loop(0, ROW_TM, issue, 0, unroll=8)

    @pl.when(i + 1 < n)
    def _():
        _idx_copy(idn_ref, idx_smem, 1 - slot, isem).wait()

    @pl.when(i > 0)
    def _():
        drain(1 - slot)

    @pl.when(i + 1 == n)
    def _():
        drain(slot)


def _dispatch(h2p, dest_idx, zstart, n_used, n_slots):
    t = h2p.shape[0]
    nt = t // ROW_TM
    grid_spec = pltpu.PrefetchScalarGridSpec(
        num_scalar_prefetch=2,
        grid=(nt,),
        in_specs=[pl.BlockSpec((1, 1, TOP_K * ROW_TM), lambda i, *_: (i, 0, 0)),
                  pl.BlockSpec((1, 1, TOP_K * ROW_TM), lambda i, *_: (jnp.minimum(i + 1, nt - 1), 0, 0)),
                  pl.BlockSpec(memory_space=pl.ANY)],
        out_specs=pl.BlockSpec(memory_space=pl.ANY),
        scratch_shapes=[pltpu.SMEM((2, TOP_K * ROW_TM), I32), pltpu.VMEM((MOE_TM,) + ROW_TILE, U32),
                        pltpu.SemaphoreType.DMA((2,)), pltpu.SemaphoreType.DMA, pltpu.SemaphoreType.DMA],
    )
    return pl.pallas_call(
        _dispatch_kernel,
        out_shape=jax.ShapeDtypeStruct((n_slots,) + ROW_TILE, U32),
        grid_spec=grid_spec,
        compiler_params=_cparams(("arbitrary",), 16),
        name="moe_dispatch",
    )(zstart, n_used, dest_idx, dest_idx, h2p)


def _expert_kernel(be_ref, nused_ref, x_ref, wg_ref, wu_ref, wd_ref, out_ref):
    i = pl.program_id(0)

    @pl.when(i < nused_ref[0])
    def _():
        x_lo, x_hi = _unpack_pair(_tiles_to_rows(x_ref[...]))
        x_lo = x_lo.astype(BF16)
        x_hi = x_hi.astype(BF16)
        g = (jnp.dot(x_lo, wg_ref[:HALF, :], preferred_element_type=F32)
             + jnp.dot(x_hi, wg_ref[HALF:, :], preferred_element_type=F32))
        u = (jnp.dot(x_lo, wu_ref[:HALF, :], preferred_element_type=F32)
             + jnp.dot(x_hi, wu_ref[HALF:, :], preferred_element_type=F32))
        hmid = (_silu(g) * u).astype(BF16)
        y = jnp.dot(hmid, wd_ref[...], preferred_element_type=F32)
        out_ref[...] = _rows_to_tiles(_pack_pair(y[:, :HALF], y[:, HALF:]))

    @pl.when(i >= nused_ref[0])
    def _():
        out_ref[...] = jnp.zeros_like(out_ref)


def _experts(xs, block_expert, n_used, wg, wu, wd, layer):
    n_slots = xs.shape[0]
    n_blocks = n_slots // MOE_TM
    blk = lambda i, be, nu: (jnp.minimum(i, nu[0] - 1), 0, 0)
    wmap = lambda i, be, nu: (layer, be[i], 0, 0)
    grid_spec = pltpu.PrefetchScalarGridSpec(
        num_scalar_prefetch=2,
        grid=(n_blocks,),
        in_specs=[
            pl.BlockSpec((MOE_TM,) + ROW_TILE, blk),
            pl.BlockSpec((None, None, D_MODEL, D_FF_EXPERT), wmap),
            pl.BlockSpec((None, None, D_MODEL, D_FF_EXPERT), wmap),
            pl.BlockSpec((None, None, D_FF_EXPERT, D_MODEL), wmap),
        ],
        out_specs=pl.BlockSpec((MOE_TM,) + ROW_TILE, lambda i, be, nu: (i, 0, 0)),
    )
    return pl.pallas_call(
        _expert_kernel,
        out_shape=jax.ShapeDtypeStruct((n_slots,) + ROW_TILE, U32),
        grid_spec=grid_spec,
        compiler_params=_cparams(("arbitrary",), 48),
        name="moe_experts",
    )(block_expert, n_used, xs, wg, wu, wd)


def _combine_kernel(id0_ref, id1_ref, idn_ref, x1_ref, gate_ref, ys_hbm, nw_ref, out_ref, idx_smem, buf_a, buf_b,
                    sem, isem, *, final_norm):
    i = pl.program_id(0)
    n = pl.num_programs(0)
    slot = i % 2

    def row(tile_slot, r, k, src_row):
        buf = buf_a if k == 0 else buf_b
        return pltpu.make_async_copy(ys_hbm.at[pl.ds(src_row, 1)], buf.at[tile_slot, pl.ds(r, 1)],
                                     sem.at[tile_slot])

    def issue_rows(tile_slot):
        def body(r, carry):
            row(tile_slot, r, 0, idx_smem[tile_slot, r]).start(priority=0)
            row(tile_slot, r, 1, idx_smem[tile_slot, ROW_TM + r]).start(priority=1)
            return carry
        lax.fori_loop(0, ROW_TM, body, 0, unroll=8)

    @pl.when(i == 0)
    def _():
        cp = _idx_copy(id0_ref, idx_smem, 0, isem)
        cp.start()
        cp.wait()
        issue_rows(0)

        @pl.when(n > 1)
        def _():
            cp1 = _idx_copy(id1_ref, idx_smem, 1, isem)
            cp1.start()
            cp1.wait()

    @pl.when(i + 1 < n)
    def _():
        issue_rows(1 - slot)

    @pl.when(i + 2 < n)
    def _():
        _idx_copy(idn_ref, idx_smem, slot, isem).start()

    def drain(r, carry):
        row(slot, r, 0, 0).wait()
        row(slot, r, 1, 0).wait()
        return carry

    lax.fori_loop(0, ROW_TM, drain, 0, unroll=8)

    @pl.when(i + 2 < n)
    def _():
        _idx_copy(idn_ref, idx_smem, slot, isem).wait()

    a_lo, a_hi = _unpack_pair(_tiles_to_rows(buf_a[slot]))
    b_lo, b_hi = _unpack_pair(_tiles_to_rows(buf_b[slot]))
    g1 = gate_ref[:, 0:1]
    g2 = gate_ref[:, 1:2]
    o_lo = x1_ref[:, :HALF] + (a_lo * g1 + b_lo * g2)
    o_hi = x1_ref[:, HALF:] + (a_hi * g1 + b_hi * g2)
    if final_norm:
        ms = (jnp.sum(o_lo * o_lo, axis=-1, keepdims=True) + jnp.sum(o_hi * o_hi, axis=-1, keepdims=True)) / D_MODEL
        scale = lax.rsqrt(ms + EPS)
        o_lo = (o_lo * scale) * nw_ref[:, :HALF]
        o_hi = (o_hi * scale) * nw_ref[:, HALF:]
    out_ref[:, :HALF] = o_lo
    out_ref[:, HALF:] = o_hi


def _combine(x1, gate, ys, dest_idx, nw, *, tile0, n_tiles, final_norm):
    last = tile0 + n_tiles - 1
    idx_block = (1, 1, TOP_K * ROW_TM)
    return pl.pallas_call(
        functools.partial(_combine_kernel, final_norm=final_norm),
        out_shape=jax.ShapeDtypeStruct((n_tiles * ROW_TM, D_MODEL), F32),
        grid=(n_tiles,),
        in_specs=[pl.BlockSpec(idx_block, lambda i: (tile0, 0, 0)),
                  pl.BlockSpec(idx_block, lambda i: (min(tile0 + 1, last), 0, 0)),
                  pl.BlockSpec(idx_block, lambda i: (jnp.minimum(tile0 + i + 2, last), 0, 0)),
                  pl.BlockSpec((ROW_TM, D_MODEL), lambda i: (tile0 + i, 0)),
                  pl.BlockSpec((ROW_TM, LANES), lambda i: (tile0 + i, 0)),
                  pl.BlockSpec(memory_space=pl.ANY),
                  pl.BlockSpec((1, D_MODEL), lambda i: (0, 0))],
        out_specs=pl.BlockSpec((ROW_TM, D_MODEL), lambda i: (i, 0)),
        scratch_shapes=[pltpu.SMEM((2, TOP_K * ROW_TM), I32), pltpu.VMEM((2, ROW_TM) + ROW_TILE, U32),
                        pltpu.VMEM((2, ROW_TM) + ROW_TILE, U32), pltpu.SemaphoreType.DMA((2,)),
                        pltpu.SemaphoreType.DMA],
        compiler_params=_cparams(("arbitrary",), 24),
        name="moe_combine",
    )(dest_idx, dest_idx, dest_idx, x1, gate, ys, nw)


def _dispatch_plan(eid, rank, counts):
    t = eid.shape[0]
    n_assign = t * TOP_K
    padded = (counts + MOE_TM - 1) // MOE_TM * MOE_TM
    pend = jnp.cumsum(padded)
    pstart = pend - padded
    dest = pstart[eid] + rank
    n_blocks = -(-n_assign // MOE_TM) + N_EXPERTS
    n_slots = n_blocks * MOE_TM
    block_start = jnp.arange(n_blocks, dtype=I32) * MOE_TM
    block_expert = jnp.minimum(jnp.sum((pend[None, :] <= block_start[:, None]).astype(I32), axis=1),
                               N_EXPERTS - 1).astype(I32)
    n_used = (pend[-1] // MOE_TM).astype(I32).reshape(1)
    zstart = jnp.minimum((pstart + counts) // SUBLANES * SUBLANES, n_slots - MOE_TM).astype(I32)
    dest_idx = dest.reshape(t // ROW_TM, ROW_TM, TOP_K).transpose(0, 2, 1).reshape(t // ROW_TM, 1, TOP_K * ROW_TM)
    return dest_idx.astype(I32), block_expert, n_used, zstart, n_slots


def _seq_flags(seq_lens, tile):
    first, last = [], []
    for s in seq_lens:
        assert s % tile == 0
        n = s // tile
        first += [1] + [0] * (n - 1)
        last += [0] * (n - 1) + [1]
    return jnp.asarray(first, I32), jnp.asarray(last, I32)


def _rope_tables(seq_lens):
    pos = jnp.concatenate([jnp.arange(s, dtype=F32) for s in seq_lens])
    inv_freq = ROPE_THETA ** (-jnp.arange(0, ROPE_DIM, 2, dtype=F32) / ROPE_DIM)
    ang = pos[:, None] * inv_freq[None, :]
    d = jnp.arange(LANES) % HEAD_DIM
    cos = jnp.cos(ang)[:, d % ROPE_HALF]
    sin = jnp.sin(ang)[:, d % ROPE_HALF]
    c = jnp.where(d < ROPE_DIM, cos, 1.0)
    sa = jnp.where((d >= ROPE_HALF) & (d < ROPE_DIM), sin, 0.0)
    sb = jnp.where(d < ROPE_HALF, -sin, 0.0)
    return c, sa, sb


def _head_expand(d0):
    rows = jnp.arange(LANES)[:, None]
    cols = jnp.arange(SSM_WIDTH)[None, :] // SSM_HEAD_DIM
    return (rows == cols + d0).astype(BF16)


def _pad_lanes(v, width=LANES):
    v = v.reshape(1, -1)
    return jnp.pad(v, ((0, 0), (0, width - v.shape[1])))


def _trunk(xs, seq_lens, attn_norm_w, w_in, conv_w, conv_b, attn_sink, attn_out_norm_w, ssm_a_log, ssm_dt_bias,
           ssm_d, ssm_norm_w, w_out, ffn_norm_w, w_router_group, b_router_group, w_router_expert,
           b_router_expert, w_gate, w_up, w_down, final_norm_w, out_tiles):
    depth = w_in.shape[0]
    t = sum(seq_lens)
    rope_c, rope_sa, rope_sb = _rope_tables(seq_lens)
    tile_first, tile_last = _seq_flags(seq_lens, ATTN_TQ)
    chunk_first, chunk_last = _seq_flags(seq_lens, CHUNK)
    e_fwd, e_bwd = _head_expand(0), _head_expand(SSM_HEADS)
    shift = _conv_shift_matrix()
    tri = (jnp.arange(ROUTE_TM)[:, None] > jnp.arange(ROUTE_TM)[None, :]).astype(BF16)
    z_end = ATTN_WIDTH + 2 * KV_WIDTH + SSM_WIDTH
    w_out_b, w_gate_b, w_up_b, w_down_b = (v.astype(BF16) for v in (w_out, w_gate, w_up, w_down))
    outs = None
    for l in range(depth):
        w = w_in[l]
        w_main = jnp.concatenate([w[:, :ATTN_WIDTH + 2 * KV_WIDTH], w[:, z_end:z_end + CONV_CH],
                                  w[:, ATTN_WIDTH + 2 * KV_WIDTH:z_end]], axis=1).astype(BF16)
        w_dt = jnp.pad(w[:, z_end + CONV_CH:], ((0, 0), (0, LANES - 2 * SSM_HEADS))).astype(BF16)
        proj, dt_raw = _proj(xs, attn_norm_w[l].reshape(1, -1), w_main, w_dt, rope_c, rope_sa, rope_sb)
        attn = _attention(proj, tile_first, tile_last, attn_sink[l] * LOG2E, attn_out_norm_w[l].reshape(1, -1))
        cw = jnp.pad(conv_w[l], ((0, SUBLANES - CONV_WIDTH), (0, 0)))
        cb = conv_b[l].reshape(1, -1)
        dtb = _pad_lanes(ssm_dt_bias[l])
        alog = _pad_lanes(ssm_a_log[l])
        dexp = jnp.repeat(ssm_d[l], SSM_HEAD_DIM).reshape(1, -1)
        y_fwd, u_fwd = _ssd(proj, dt_raw, chunk_first, chunk_last, dtb, alog, e_fwd, dexp, backward=False,
                            conv=(shift, cw, cb))
        ssd = _ssd(proj, dt_raw, chunk_first, chunk_last, dtb, alog, e_bwd, ssm_norm_w[l].reshape(1, -1),
                   backward=True, u_fwd=u_fwd, y_fwd=y_fwd)
        wr = jnp.pad(jnp.concatenate([w_router_group[l], w_router_expert[l]], axis=1),
                     ((0, 0), (0, LANES - N_EXPERT_GROUPS - N_EXPERTS)))
        wr_hi, wr_lo = _split2(wr)
        br = _pad_lanes(jnp.concatenate([b_router_group[l], b_router_expert[l]]))
        x1, h2p, logits = _outproj(attn, ssd, xs, w_out_b, l, ffn_norm_w[l].reshape(1, -1),
                                   jnp.concatenate([wr_hi, wr_lo], axis=1), wr_hi, br)
        eid, gate, cnt = _route(logits, tri)
        counts = cnt[0, :N_EXPERTS].astype(I32)
        dest_idx, block_expert, n_used, zstart, n_slots = _dispatch_plan(eid[:, :TOP_K], eid[:, TOP_K:2 * TOP_K],
                                                                        counts)
        xslots = _dispatch(h2p, dest_idx, zstart, n_used, n_slots)
        ys = _experts(xslots, block_expert, n_used, w_gate_b, w_up_b, w_down_b, l)
        nw = final_norm_w.reshape(1, -1)
        if l == depth - 1:
            outs, tile0 = [], 0
            for nt in out_tiles:
                outs.append(_combine(x1, gate, ys, dest_idx, nw, tile0=tile0, n_tiles=nt, final_norm=True))
                tile0 += nt
        else:
            xs = [_combine(x1, gate, ys, dest_idx, nw, tile0=0, n_tiles=t // ROW_TM, final_norm=False)]
    return outs


def kernel(x_prompt, x_sample, attn_norm_w, w_in, conv_w, conv_b, attn_sink, attn_out_norm_w, ssm_a_log,
           ssm_dt_bias, ssm_d, ssm_norm_w, w_out, ffn_norm_w, w_router_group, b_router_group, w_router_expert,
           b_router_expert, w_gate, w_up, w_down, final_norm_w):
    bp, sp, d = x_prompt.shape
    bs, ss, _ = x_sample.shape
    seq_lens = [sp] * bp + [ss] * bs
    xs = [x_prompt.reshape(bp * sp, d), x_sample.reshape(bs * ss, d)]
    y_prompt, y_sample = _trunk(xs, seq_lens, attn_norm_w, w_in, conv_w, conv_b, attn_sink, attn_out_norm_w,
                                ssm_a_log, ssm_dt_bias, ssm_d, ssm_norm_w, w_out, ffn_norm_w, w_router_group,
                                b_router_group, w_router_expert, b_router_expert, w_gate, w_up, w_down,
                                final_norm_w, out_tiles=[bp * sp // ROW_TM, bs * ss // ROW_TM])
    return (y_prompt.reshape(bp, sp, d), y_sample.reshape(bs, ss, d))
```

```python
import functools
import math

import numpy as np
import jax
import jax.numpy as jnp
from jax import lax
from jax.experimental import pallas as pl
from jax.experimental.pallas import tpu as pltpu

D_MODEL = 2048
N_Q_HEADS = 16
N_KV_HEADS = 4
HEAD_DIM = 64
Q_GROUP = N_Q_HEADS // N_KV_HEADS
ATTN_WIDTH = N_Q_HEADS * HEAD_DIM
KV_WIDTH = N_KV_HEADS * HEAD_DIM
WINDOW = 128
BLOCK = 128
ROPE_THETA = 500000.0
ROPE_DIM = HEAD_DIM // 4
ROPE_HALF = ROPE_DIM // 2
SSM_HEADS = 16
SSM_HEAD_DIM = 64
SSM_WIDTH = SSM_HEADS * SSM_HEAD_DIM
SSM_GROUPS = 2
HEADS_PER_GROUP = SSM_HEADS // SSM_GROUPS
D_STATE = 128
BC_WIDTH = SSM_GROUPS * D_STATE
CONV_WIDTH = 5
CONV_PAD = CONV_WIDTH // 2
CONV_CH = SSM_WIDTH + 2 * BC_WIDTH
CHUNK = 128
N_EXPERT_GROUPS = 4
EXPERTS_PER_GROUP = 8
N_EXPERTS = N_EXPERT_GROUPS * EXPERTS_PER_GROUP
TOP_K = 2
D_FF_EXPERT = 1024
EPS = 1e-6

LANES = 128
SUBLANES = 8
BF16_SUBLANES = 16
MIB = 1024 * 1024

COL_Q = 0
COL_K = ATTN_WIDTH
COL_V = COL_K + KV_WIDTH
COL_XBC = COL_V + KV_WIDTH
COL_Z = COL_XBC + CONV_CH
PROJ_MAIN = COL_Z + SSM_WIDTH
PROJ_TM = 512
PROJ_TN = 1024
ATTN_TQ = 512
OUT_TM = 256
ROUTE_TM = 512
SSD_NCH = 4
MOE_TM = 256
ROW_TM = 256
CONV_K = 2 * CHUNK
NEG = -1e30
LOG2E = math.log2(math.e)
Q_SCALE = HEAD_DIM ** -0.5 * LOG2E
F32 = jnp.float32
BF16 = jnp.bfloat16
I32 = jnp.int32


def _cparams(semantics, vmem_mib):
    return pltpu.CompilerParams(dimension_semantics=semantics, vmem_limit_bytes=vmem_mib * MIB)


def _rms(x, w):
    ms = jnp.mean(x * x, axis=-1, keepdims=True)
    return (x * lax.rsqrt(ms + EPS)) * w


def _silu(x):
    return x / (1.0 + jnp.exp(-x))


def _split2(v):
    hi = v.astype(BF16)
    lo = (v - hi.astype(F32)).astype(BF16)
    return hi, lo


def _split3(v):
    hi = v.astype(BF16)
    r = v - hi.astype(F32)
    mid = r.astype(BF16)
    lo = (r - mid.astype(F32)).astype(BF16)
    return hi, mid, lo


ROW_TILE = (BF16_SUBLANES, D_MODEL // BF16_SUBLANES)


def _rows_to_tiles(v):
    return v.astype(BF16).reshape((v.shape[0],) + ROW_TILE)


def _tiles_to_rows(v):
    return v.reshape(v.shape[0], D_MODEL)


def _select_src(i, x_refs, src_tiles):
    if len(x_refs) == 1:
        return x_refs[0][...]
    return jnp.where(i < src_tiles, x_refs[0][...], x_refs[1][...])


def _src_specs(xs, tm, width, nargs):
    if len(xs) == 1:
        return [pl.BlockSpec((tm, width), lambda i, *_: (i, 0))], 0
    n0 = xs[0].shape[0] // tm
    return [pl.BlockSpec((tm, width), lambda i, *_: (jnp.minimum(i, n0 - 1), 0)),
            pl.BlockSpec((tm, width), lambda i, *_: (jnp.maximum(i - n0, 0), 0))], n0


def _proj_kernel(*refs, n_src, src_tiles):
    x_refs = refs[:n_src]
    nw_ref, w_ref, wdt_ref, c_ref, sa_ref, sb_ref, out_ref, dt_ref = refs[n_src:]
    i = pl.program_id(0)
    h = _rms(_select_src(i, x_refs, src_tiles), nw_ref[...]).astype(BF16)
    dt_ref[...] = jnp.dot(h, wdt_ref[...], preferred_element_type=F32)

    def rope(a):
        return (a * c_ref[...] + pltpu.roll(a, ROPE_HALF, 1) * sa_ref[...]
                + pltpu.roll(a, LANES - ROPE_HALF, 1) * sb_ref[...])

    for jt in range(PROJ_MAIN // PROJ_TN):
        c0 = jt * PROJ_TN
        acc = jnp.dot(h, w_ref[:, c0:c0 + PROJ_TN], preferred_element_type=F32)
        for cb in range(PROJ_TN // LANES):
            col = c0 + cb * LANES
            a = acc[:, cb * LANES:(cb + 1) * LANES]
            if col < COL_V:
                a = rope(a)
            if col < COL_K:
                a = a * Q_SCALE
            out_ref[:, col:col + LANES] = a.astype(BF16)


def _proj(xs, nw, w_main, w_dt, rope_c, rope_sa, rope_sb):
    t = sum(x.shape[0] for x in xs)
    assert t % PROJ_TM == 0 and all(x.shape[0] % PROJ_TM == 0 for x in xs)
    row = lambda i: (i, 0)
    const = lambda i: (0, 0)
    once = dict(pipeline_mode=pl.Buffered(1))
    src_specs, n0 = _src_specs(xs, PROJ_TM, D_MODEL, 1)
    return pl.pallas_call(
        functools.partial(_proj_kernel, n_src=len(xs), src_tiles=n0),
        out_shape=(jax.ShapeDtypeStruct((t, PROJ_MAIN), BF16), jax.ShapeDtypeStruct((t, LANES), F32)),
        grid=(t // PROJ_TM,),
        in_specs=src_specs + [
            pl.BlockSpec((1, D_MODEL), const),
            pl.BlockSpec((D_MODEL, PROJ_MAIN), const, **once),
            pl.BlockSpec((D_MODEL, LANES), const, **once),
            pl.BlockSpec((PROJ_TM, LANES), row),
            pl.BlockSpec((PROJ_TM, LANES), row),
            pl.BlockSpec((PROJ_TM, LANES), row),
        ],
        out_specs=(pl.BlockSpec((PROJ_TM, PROJ_MAIN), row), pl.BlockSpec((PROJ_TM, LANES), row)),
        compiler_params=_cparams(("arbitrary",), 56),
        name="proj",
    )(*xs, nw, w_main, w_dt, rope_c, rope_sa, rope_sb)


def _attn_kernel(tf_ref, tl_ref, sink_ref, q_ref, kc_ref, vc_ref, kp_ref, vp_ref, kn_ref, vn_ref,
                 nw_ref, o_ref):
    i = pl.program_id(0)
    kw = jnp.concatenate([kp_ref[...], kc_ref[...], kn_ref[...]], axis=0)
    vw = jnp.concatenate([vp_ref[...], vc_ref[...], vn_ref[...]], axis=0)
    rows = kw.shape[0]
    lane_w = lax.broadcasted_iota(I32, (rows, LANES), 1)
    low = lane_w < HEAD_DIM
    def pad_pair(slab, g):
        s = slab.astype(F32)
        if g % 2 == 0:
            e = jnp.where(low, s, 0.0)
            o = pltpu.roll(e, HEAD_DIM, 1)
        else:
            o = jnp.where(low, 0.0, s)
            e = pltpu.roll(o, HEAD_DIM, 1)
        return e, o

    ke, ko, vx = [], [], []
    for g in range(N_KV_HEADS):
        c0 = (g // 2) * LANES
        e, o = pad_pair(kw[:, c0:c0 + LANES], g)
        ke.append(e.astype(BF16))
        ko.append(o.astype(BF16))
        e, _ = pad_pair(vw[:, c0:c0 + LANES], g)
        vx.append(jnp.where(lane_w == HEAD_DIM, 1.0, e).astype(BF16))

    qi = lax.broadcasted_iota(I32, (BLOCK, BLOCK), 0)
    kk = lax.broadcasted_iota(I32, (BLOCK, BLOCK), 1)
    bias_prev = jnp.where(kk >= qi, 0.0, NEG)
    bias_next = jnp.where(kk <= qi, 0.0, NEG)
    lane_b = lax.broadcasted_iota(I32, (BLOCK, LANES), 1)
    nsb = ATTN_TQ // BLOCK
    for sb in range(nsb):
        bp, bn = bias_prev, bias_next
        if sb == 0:
            bp = jnp.where(tf_ref[i] == 1, NEG, bias_prev)
        if sb == nsb - 1:
            bn = jnp.where(tl_ref[i] == 1, NEG, bias_next)
        r0 = sb * BLOCK
        slabs = []
        for g in range(N_KV_HEADS):
            q0 = g * Q_GROUP * HEAD_DIM
            lhs = jnp.concatenate([q_ref[r0:r0 + BLOCK, q0:q0 + LANES],
                                   q_ref[r0:r0 + BLOCK, q0 + LANES:q0 + 2 * LANES]], axis=0)
            rhs = jnp.concatenate([ke[g][r0:r0 + 3 * BLOCK], ko[g][r0:r0 + 3 * BLOCK]], axis=0)
            s = lax.dot_general(lhs, rhs, (((1,), (1,)), ((), ())), preferred_element_type=F32)
            ps, ms = [], []
            for hi in range(Q_GROUP):
                c0 = (hi % 2) * 3 * BLOCK
                sq = s[(hi // 2) * BLOCK:(hi // 2 + 1) * BLOCK, c0:c0 + 3 * BLOCK]
                s0 = sq[:, :BLOCK] + bp
                s1 = sq[:, BLOCK:2 * BLOCK]
                s2 = sq[:, 2 * BLOCK:] + bn
                m = jnp.max(jnp.maximum(jnp.maximum(s0, s1), s2), axis=-1, keepdims=True)
                m = jnp.maximum(m, sink_ref[g * Q_GROUP + hi])
                ps.append(jnp.concatenate([jnp.exp2(s0 - m).astype(BF16), jnp.exp2(s1 - m).astype(BF16),
                                           jnp.exp2(s2 - m).astype(BF16)], axis=1))
                ms.append(m)
            o4 = jnp.dot(jnp.concatenate(ps, axis=0), vx[g][r0:r0 + 3 * BLOCK], preferred_element_type=F32)
            normed = []
            for hi in range(Q_GROUP):
                oh = o4[hi * BLOCK:(hi + 1) * BLOCK]
                den = oh[:, HEAD_DIM:HEAD_DIM + 1] + jnp.exp2(sink_ref[g * Q_GROUP + hi] - ms[hi])
                normed.append(oh * (1.0 / den))
            for half in range(2):
                slabs.append(jnp.where(lane_b < HEAD_DIM, normed[2 * half],
                                       pltpu.roll(normed[2 * half + 1], HEAD_DIM, 1)))
        o = jnp.concatenate(slabs, axis=1)
        o_ref[r0:r0 + BLOCK, :] = _rms(o, nw_ref[...]).astype(BF16)


def _attention(proj, tile_first, tile_last, sink2, nw):
    t = proj.shape[0]
    assert t % ATTN_TQ == 0
    nb = t // BLOCK
    r = ATTN_TQ // BLOCK
    kcol, vcol = COL_K // KV_WIDTH, COL_V // KV_WIDTH
    prev = lambda i: jnp.maximum(i * r - 1, 0)
    nxt = lambda i: jnp.minimum(i * r + r, nb - 1)
    grid_spec = pltpu.PrefetchScalarGridSpec(
        num_scalar_prefetch=3,
        grid=(t // ATTN_TQ,),
        in_specs=[
            pl.BlockSpec((ATTN_TQ, ATTN_WIDTH), lambda i, *_: (i, COL_Q // ATTN_WIDTH)),
            pl.BlockSpec((ATTN_TQ, KV_WIDTH), lambda i, *_: (i, kcol)),
            pl.BlockSpec((ATTN_TQ, KV_WIDTH), lambda i, *_: (i, vcol)),
            pl.BlockSpec((BLOCK, KV_WIDTH), lambda i, *_: (prev(i), kcol)),
            pl.BlockSpec((BLOCK, KV_WIDTH), lambda i, *_: (prev(i), vcol)),
            pl.BlockSpec((BLOCK, KV_WIDTH), lambda i, *_: (nxt(i), kcol)),
            pl.BlockSpec((BLOCK, KV_WIDTH), lambda i, *_: (nxt(i), vcol)),
            pl.BlockSpec((1, ATTN_WIDTH), lambda i, *_: (0, 0)),
        ],
        out_specs=pl.BlockSpec((ATTN_TQ, ATTN_WIDTH), lambda i, *_: (i, 0)),
    )
    return pl.pallas_call(
        _attn_kernel,
        out_shape=jax.ShapeDtypeStruct((t, ATTN_WIDTH), BF16),
        grid_spec=grid_spec,
        compiler_params=_cparams(("arbitrary",), 32),
        name="attn",
    )(tile_first, tile_last, sink2, proj, proj, proj, proj, proj, proj, proj, nw)


def _ssd_kernel(first_ref, last_ref, *refs, backward):
    if backward:
        xc_ref, dt_ref, *rest = refs
        xp_ref = xn_ref = None
    else:
        xc_ref, xp_ref, xn_ref, dt_ref, *rest = refs
    h_ref = rest[-1]
    i = pl.program_id(0)
    tile = (pl.num_programs(0) - 1 - i) if backward else i
    order = range(SSD_NCH - 1, -1, -1) if backward else range(SSD_NCH)

    @pl.when(i == 0)
    def _():
        h_ref[...] = jnp.zeros_like(h_ref)

    h = h_ref[...]
    for k in order:
        h = _ssd_chunk(k, tile * SSD_NCH + k, h, first_ref, last_ref, xc_ref, xp_ref, xn_ref, dt_ref, rest,
                       backward)
    h_ref[...] = h


def _ssd_chunk(k, c, hprev, first_ref, last_ref, xc_ref, xp_ref, xn_ref, dt_ref, rest, backward):
    is_first = first_ref[c] == 1
    is_last = last_ref[c] == 1
    hprev = jnp.where(is_last if backward else is_first, jnp.zeros_like(hprev), hprev)
    r0 = k * CHUNK
    rows = pl.ds(r0, CHUNK)

    if backward:
        (z_ref, yf_ref, dtb_ref, alog_ref, e_ref, nw_ref, out_ref, _) = rest
        u = xc_ref[rows, :].astype(F32)
    else:
        (sh_ref, cw_ref, cb_ref, dtb_ref, alog_ref, e_ref, dexp_ref, out_ref, u_ref, _) = rest
        halo = BF16_SUBLANES
        xc = xc_ref[rows, :]
        no_rows = jnp.zeros((halo, CONV_CH), BF16)
        before = xp_ref[...] if k == 0 else xc_ref[r0 - halo:r0, :]
        after = xn_ref[...] if k == SSD_NCH - 1 else xc_ref[r0 + CHUNK:r0 + CHUNK + halo, :]
        xp = jnp.where(is_first, no_rows, before)
        xn = jnp.where(is_last, no_rows, after)
        xw = jnp.concatenate([xc, xp, xn, jnp.zeros((CONV_K - CHUNK - 2 * halo, CONV_CH), BF16)], axis=0)
        shifted = jnp.dot(sh_ref[...], xw, preferred_element_type=F32)
        acc = cb_ref[...] + xc.astype(F32) * cw_ref[CONV_PAD:CONV_PAD + 1, :]
        taps = [tap for tap in range(CONV_WIDTH) if tap != CONV_PAD]
        for n, tap in enumerate(taps):
            acc = acc + shifted[n * CHUNK:(n + 1) * CHUNK] * cw_ref[tap:tap + 1, :]
        u = _silu(acc)
        u_ref[rows, :] = u.astype(BF16)
    xs = u[:, :SSM_WIDTH]
    bm = u[:, SSM_WIDTH:SSM_WIDTH + BC_WIDTH]
    cm = u[:, SSM_WIDTH + BC_WIDTH:]

    d0 = SSM_HEADS if backward else 0
    draw = dt_ref[rows, :] + dtb_ref[...]
    dt = jnp.maximum(draw, 0.0) + jnp.log1p(jnp.exp(-jnp.abs(draw)))
    a = dt * (-jnp.exp(alog_ref[...]))
    li = lax.broadcasted_iota(I32, (CHUNK, CHUNK), 0)
    ti = lax.broadcasted_iota(I32, (CHUNK, CHUNK), 1)
    causal = (li <= ti) if backward else (li >= ti)
    tri = jnp.where(causal, 1.0, 0.0).astype(BF16)
    a_hi, a_mid, a_lo = _split3(a)
    acum = (jnp.dot(tri, a_hi, preferred_element_type=F32) + jnp.dot(tri, a_mid, preferred_element_type=F32)
            + jnp.dot(tri, a_lo, preferred_element_type=F32))
    end = 0 if backward else CHUNK - 1
    tot = acum[end:end + 1, :]
    exp_a = jnp.exp(acum)
    wdec = dt * jnp.exp(tot - acum)
    stack = jnp.concatenate(_split2(exp_a) + (wdec.astype(BF16),), axis=0)
    ex = jnp.dot(stack, e_ref[...], preferred_element_type=F32)
    ea_e = ex[0:CHUNK] + ex[CHUNK:2 * CHUNK]
    wd_e = ex[2 * CHUNK:3 * CHUNK]
    xd = (xs * wd_e).astype(BF16)
    acum_t = (acum - jnp.log(dt)).T
    hb = hprev.astype(BF16)
    lane = lax.broadcasted_iota(I32, (CHUNK, LANES), 1)
    gw = HEADS_PER_GROUP * SSM_HEAD_DIM
    ys = []
    sts = []
    for g in range(SSM_GROUPS):
        bg = bm[:, g * D_STATE:(g + 1) * D_STATE]
        cg = cm[:, g * D_STATE:(g + 1) * D_STATE].astype(BF16)
        cb = lax.dot_general(cg, bg.astype(BF16), (((1,), (1,)), ((), ())), preferred_element_type=F32)
        yoff = jnp.dot(cg, hb[:, g * gw:(g + 1) * gw], preferred_element_type=F32)
        sts.append(jnp.dot(bg.T.astype(BF16), xd[:, g * gw:(g + 1) * gw], preferred_element_type=F32))
        for pr in range(HEADS_PER_GROUP // 2):
            h0 = g * HEADS_PER_GROUP + 2 * pr
            ms = []
            for hh in (h0, h0 + 1):
                col = acum[:, d0 + hh:d0 + hh + 1]
                row = acum_t[d0 + hh:d0 + hh + 1, :]
                decay = jnp.exp(jnp.where(causal, col - row, NEG))
                ms.append((cb * decay).astype(BF16))
            lhs = jnp.concatenate(ms, axis=1)
            c0 = (h0 // 2) * LANES
            slab = xs[:, c0:c0 + LANES]
            rhs = jnp.concatenate([jnp.where(lane < SSM_HEAD_DIM, slab, 0.0),
                                   jnp.where(lane >= SSM_HEAD_DIM, slab, 0.0)], axis=0).astype(BF16)
            yd = jnp.dot(lhs, rhs, preferred_element_type=F32)
            ys.append(yd + yoff[:, pr * LANES:(pr + 1) * LANES] * ea_e[:, c0:c0 + LANES])
    y = jnp.concatenate(ys, axis=1)
    if backward:
        yt = yf_ref[rows, :] + y
        gz = yt * _silu(z_ref[rows, :].astype(F32))
        out_ref[rows, :] = _rms(gz, nw_ref[...]).astype(BF16)
    else:
        out_ref[rows, :] = y + xs * dexp_ref[...]
    return hprev * ea_e[end:end + 1, :] + jnp.concatenate(sts, axis=1)


def _conv_shift_matrix():
    halo = BF16_SUBLANES
    taps = [k for k in range(CONV_WIDTH) if k != CONV_PAD]
    m = np.zeros((len(taps) * CHUNK, CONV_K), np.float32)
    for n, k in enumerate(taps):
        for t in range(CHUNK):
            j = t + k - CONV_PAD
            if j < 0:
                col = CHUNK + halo + j
            elif j >= CHUNK:
                col = CHUNK + halo + (j - CHUNK)
            else:
                col = j
            m[n * CHUNK + t, col] = 1.0
    return jnp.asarray(m, BF16)


def _ssd(proj, dt_raw, chunk_first, chunk_last, dt_bias, a_log, expand, extra, *, backward, conv=None,
         u_fwd=None, y_fwd=None):
    t = proj.shape[0]
    tm = SSD_NCH * CHUNK
    assert t % tm == 0
    nc = t // tm
    hb = tm // BF16_SUBLANES
    n_halo = t // BF16_SUBLANES
    xcol = COL_XBC // CONV_CH
    ch = (lambda i: nc - 1 - i) if backward else (lambda i: i)
    const = lambda i, *_: (0, 0)
    tile = lambda width, col=0: pl.BlockSpec((tm, width), lambda i, *_: (ch(i), col))
    small = [pl.BlockSpec((1, LANES), const), pl.BlockSpec((1, LANES), const),
             pl.BlockSpec((LANES, SSM_WIDTH), const), pl.BlockSpec((1, SSM_WIDTH), const)]
    if backward:
        in_specs = [tile(CONV_CH), tile(LANES), tile(SSM_WIDTH, COL_Z // SSM_WIDTH), tile(SSM_WIDTH)] + small
        args = [u_fwd, dt_raw, proj, y_fwd, dt_bias, a_log, expand, extra]
        out_shape = jax.ShapeDtypeStruct((t, SSM_WIDTH), BF16)
        out_specs = tile(SSM_WIDTH)
    else:
        shift, conv_w, conv_b = conv
        in_specs = [
            tile(CONV_CH, xcol),
            pl.BlockSpec((BF16_SUBLANES, CONV_CH), lambda i, *_: (jnp.maximum(ch(i) * hb - 1, 0), xcol)),
            pl.BlockSpec((BF16_SUBLANES, CONV_CH), lambda i, *_: (jnp.minimum(ch(i) * hb + hb, n_halo - 1), xcol)),
            tile(LANES),
            pl.BlockSpec(shift.shape, const),
            pl.BlockSpec((SUBLANES, CONV_CH), const),
            pl.BlockSpec((1, CONV_CH), const),
        ] + small
        args = [proj, proj, proj, dt_raw, shift, conv_w, conv_b, dt_bias, a_log, expand, extra]
        out_shape = (jax.ShapeDtypeStruct((t, SSM_WIDTH), F32), jax.ShapeDtypeStruct((t, CONV_CH), BF16))
        out_specs = (tile(SSM_WIDTH), tile(CONV_CH))
    grid_spec = pltpu.PrefetchScalarGridSpec(
        num_scalar_prefetch=2,
        grid=(nc,),
        in_specs=in_specs,
        out_specs=out_specs,
        scratch_shapes=[pltpu.VMEM((D_STATE, SSM_WIDTH), F32)],
    )
    return pl.pallas_call(
        functools.partial(_ssd_kernel, backward=backward),
        out_shape=out_shape,
        grid_spec=grid_spec,
        compiler_params=_cparams(("arbitrary",), 32),
        name="ssd_bwd" if backward else "ssd_fwd",
    )(chunk_first, chunk_last, *args)


def _outproj_kernel(*refs, n_src, src_tiles):
    a_ref, s_ref = refs[:2]
    x_refs = refs[2:2 + n_src]
    wa_ref, ws_ref, nw_ref, wr1_ref, wr2_ref, br_ref, x1_ref, h2_ref, lg_ref = refs[2 + n_src:]
    i = pl.program_id(0)
    y = (jnp.dot(a_ref[...], wa_ref[...], preferred_element_type=F32)
         + jnp.dot(s_ref[...], ws_ref[...], preferred_element_type=F32))
    x1 = _select_src(i, x_refs, src_tiles) + y
    x1_ref[...] = x1
    h2 = _rms(x1, nw_ref[...])
    h2_ref[...] = _rows_to_tiles(h2)
    h_hi, h_lo = _split2(h2)
    part = jnp.dot(h_hi, wr1_ref[...], preferred_element_type=F32)
    lg_ref[...] = (part[:, :LANES] + (part[:, LANES:] + jnp.dot(h_lo, wr2_ref[...], preferred_element_type=F32))
                   + br_ref[...])


def _outproj(attn, ssd, xs, w_out, layer, nw, wr1, wr2, br):
    t = attn.shape[0]
    assert t % OUT_TM == 0
    row = lambda i: (i, 0)
    const = lambda i: (0, 0)
    once = dict(pipeline_mode=pl.Buffered(1))
    src_specs, n0 = _src_specs(xs, OUT_TM, D_MODEL, 1)
    return pl.pallas_call(
        functools.partial(_outproj_kernel, n_src=len(xs), src_tiles=n0),
        out_shape=(jax.ShapeDtypeStruct((t, D_MODEL), F32), jax.ShapeDtypeStruct((t,) + ROW_TILE, BF16),
                   jax.ShapeDtypeStruct((t, LANES), F32)),
        grid=(t // OUT_TM,),
        in_specs=[pl.BlockSpec((OUT_TM, ATTN_WIDTH), row), pl.BlockSpec((OUT_TM, SSM_WIDTH), row)] + src_specs + [
            pl.BlockSpec((None, ATTN_WIDTH, D_MODEL), lambda i: (layer, 0, 0), **once),
            pl.BlockSpec((None, SSM_WIDTH, D_MODEL), lambda i: (layer, ATTN_WIDTH // SSM_WIDTH, 0), **once),
            pl.BlockSpec((1, D_MODEL), const),
            pl.BlockSpec((D_MODEL, 2 * LANES), const, **once),
            pl.BlockSpec((D_MODEL, LANES), const, **once),
            pl.BlockSpec((1, LANES), const),
        ],
        out_specs=(pl.BlockSpec((OUT_TM, D_MODEL), row), pl.BlockSpec((OUT_TM,) + ROW_TILE, lambda i: (i, 0, 0)),
                   pl.BlockSpec((OUT_TM, LANES), row)),
        compiler_params=_cparams(("arbitrary",), 44),
        name="outproj",
    )(attn, ssd, *xs, w_out, w_out, nw, wr1, wr2, br)


def _route_kernel(lg_ref, tri_ref, eid_ref, gate_ref, cnt_ref, run_ref):
    i = pl.program_id(0)

    @pl.when(i == 0)
    def _():
        run_ref[...] = jnp.zeros_like(run_ref)

    logits = lg_ref[...]
    lane = lax.broadcasted_iota(I32, logits.shape, 1)
    big = jnp.int32(LANES)
    gl = jnp.where(lane < N_EXPERT_GROUPS, logits, NEG)
    gmax = jnp.max(gl, axis=-1, keepdims=True)
    gsel = jnp.min(jnp.where(gl == gmax, lane, big), axis=-1, keepdims=True)
    gsum = jnp.sum(jnp.exp(gl - gmax), axis=-1, keepdims=True)
    g_w = 1.0 / gsum
    e_lo = N_EXPERT_GROUPS + gsel * EXPERTS_PER_GROUP
    el = jnp.where((lane >= e_lo) & (lane < e_lo + EXPERTS_PER_GROUP), logits, NEG)
    m1 = jnp.max(el, axis=-1, keepdims=True)
    i1 = jnp.min(jnp.where(el == m1, lane, big), axis=-1, keepdims=True)
    el2 = jnp.where(lane == i1, NEG, el)
    m2 = jnp.max(el2, axis=-1, keepdims=True)
    i2 = jnp.min(jnp.where(el2 == m2, lane, big), axis=-1, keepdims=True)
    r = jnp.exp(m2 - m1)
    w1 = g_w / (1.0 + r)
    w2 = w1 * r
    e1 = i1 - N_EXPERT_GROUPS
    e2 = i2 - N_EXPERT_GROUPS
    oh1 = lane == e1
    oh2 = lane == e2
    oh = jnp.where(oh1 | oh2, 1.0, 0.0)
    before = jnp.dot(tri_ref[...], oh.astype(BF16), preferred_element_type=F32) + run_ref[0:1, :]
    rank1 = jnp.sum(jnp.where(oh1, before, 0.0), axis=-1, keepdims=True).astype(I32)
    rank2 = jnp.sum(jnp.where(oh2, before, 0.0), axis=-1, keepdims=True).astype(I32)
    run = run_ref[0:1, :] + jnp.sum(oh, axis=0, keepdims=True)
    run_ref[...] = jnp.broadcast_to(run, run_ref.shape)
    cnt_ref[...] = jnp.broadcast_to(run, cnt_ref.shape)
    eid_ref[...] = jnp.where(lane == 0, e1, jnp.where(lane == 1, e2, jnp.where(lane == 2, rank1,
                             jnp.where(lane == 3, rank2, 0))))
    gate_ref[...] = jnp.where(lane == 0, w1, jnp.where(lane == 1, w2, 0.0))


def _route(logits, tri):
    t = logits.shape[0]
    tm = tri.shape[0]
    assert t % tm == 0
    row = lambda i: (i, 0)
    const = lambda i: (0, 0)
    return pl.pallas_call(
        _route_kernel,
        out_shape=(jax.ShapeDtypeStruct((t, LANES), I32), jax.ShapeDtypeStruct((t, LANES), F32),
                   jax.ShapeDtypeStruct((SUBLANES, LANES), F32)),
        grid=(t // tm,),
        in_specs=[pl.BlockSpec((tm, LANES), row),
                  pl.BlockSpec((tm, tm), const, pipeline_mode=pl.Buffered(1))],
        out_specs=(pl.BlockSpec((tm, LANES), row), pl.BlockSpec((tm, LANES), row),
                   pl.BlockSpec((SUBLANES, LANES), const)),
        scratch_shapes=[pltpu.VMEM((SUBLANES, LANES), F32)],
        compiler_params=_cparams(("arbitrary",), 24),
        name="route",
    )(logits, tri)


def _idx_copy(src_ref, idx_smem, slot, sem):
    return pltpu.make_async_copy(src_ref.at[0], idx_smem.at[pl.ds(slot, 1)], sem)


def _dispatch_kernel(zs_ref, nused_ref, idc_ref, idn_ref, h_ref, xs_hbm, idx_smem, zbuf, hbuf, sem, isem, zsem):
    i = pl.program_id(0)
    n = pl.num_programs(0)
    slot = i % 2

    @pl.when(i == 0)
    def _():
        zbuf[...] = jnp.zeros_like(zbuf)

        def zero_block(row0):
            cp = pltpu.make_async_copy(zbuf, xs_hbm.at[pl.ds(row0, MOE_TM)], zsem)
            cp.start()
            cp.wait()

        for e in range(N_EXPERTS):
            zero_block(zs_ref[e])
        n_blocks = xs_hbm.shape[0] // MOE_TM
        for b in range(n_blocks - N_EXPERTS, n_blocks):
            @pl.when(b >= nused_ref[0])
            def _():
                zero_block(b * MOE_TM)
        cp = _idx_copy(idc_ref, idx_smem, 0, isem)
        cp.start()
        cp.wait()

    @pl.when(i + 1 < n)
    def _():
        _idx_copy(idn_ref, idx_smem, 1 - slot, isem).start()

    hbuf[slot] = h_ref[...]

    def row(r, dst_row, s):
        return pltpu.make_async_copy(hbuf.at[s, pl.ds(r, 1)], xs_hbm.at[pl.ds(dst_row, 1)], sem.at[s])

    def issue(r, carry):
        row(r, idx_smem[slot, r], slot).start(priority=0)
        row(r, idx_smem[slot, ROW_TM + r], slot).start(priority=1)
        return carry

    def drain(s):
        def body(r, carry):
            row(r, 0, s).wait()
            row(r, 0, s).wait()
            return carry
        lax.fori_loop(0, ROW_TM, body, 0, unroll=8)

    lax.fori_loop(0, ROW_TM, issue, 0, unroll=8)

    @pl.when(i + 1 < n)
    def _():
        _idx_copy(idn_ref, idx_smem, 1 - slot, isem).wait()

    @pl.when(i > 0)
    def _():
        drain(1 - slot)

    @pl.when(i + 1 == n)
    def _():
        drain(slot)


def _dispatch(h2p, dest_idx, zstart, n_used, n_slots):
    t = h2p.shape[0]
    nt = t // ROW_TM
    grid_spec = pltpu.PrefetchScalarGridSpec(
        num_scalar_prefetch=2,
        grid=(nt,),
        in_specs=[pl.BlockSpec((1, 1, TOP_K * ROW_TM), lambda i, *_: (i, 0, 0)),
                  pl.BlockSpec((1, 1, TOP_K * ROW_TM), lambda i, *_: (jnp.minimum(i + 1, nt - 1), 0, 0)),
                  pl.BlockSpec((ROW_TM,) + ROW_TILE, lambda i, *_: (i, 0, 0))],
        out_specs=pl.BlockSpec(memory_space=pl.ANY),
        scratch_shapes=[pltpu.SMEM((2, TOP_K * ROW_TM), I32), pltpu.VMEM((MOE_TM,) + ROW_TILE, BF16),
                        pltpu.VMEM((2, ROW_TM) + ROW_TILE, BF16),
                        pltpu.SemaphoreType.DMA((2,)), pltpu.SemaphoreType.DMA, pltpu.SemaphoreType.DMA],
    )
    return pl.pallas_call(
        _dispatch_kernel,
        out_shape=jax.ShapeDtypeStruct((n_slots,) + ROW_TILE, BF16),
        grid_spec=grid_spec,
        compiler_params=_cparams(("arbitrary",), 16),
        name="moe_dispatch",
    )(zstart, n_used, dest_idx, dest_idx, h2p)


def _expert_kernel(be_ref, nused_ref, x_ref, wg_ref, wu_ref, wd_ref, out_ref):
    i = pl.program_id(0)

    @pl.when(i < nused_ref[0])
    def _():
        x = _tiles_to_rows(x_ref[...])
        g = jnp.dot(x, wg_ref[...], preferred_element_type=F32)
        u = jnp.dot(x, wu_ref[...], preferred_element_type=F32)
        hmid = (_silu(g) * u).astype(BF16)
        out_ref[...] = _rows_to_tiles(jnp.dot(hmid, wd_ref[...], preferred_element_type=F32))

    @pl.when(i >= nused_ref[0])
    def _():
        out_ref[...] = jnp.zeros_like(out_ref)


def _experts(xs, block_expert, n_used, wg, wu, wd, layer):
    n_slots = xs.shape[0]
    n_blocks = n_slots // MOE_TM
    blk = lambda i, be, nu: (jnp.minimum(i, nu[0] - 1), 0, 0)
    wmap = lambda i, be, nu: (layer, be[i], 0, 0)
    grid_spec = pltpu.PrefetchScalarGridSpec(
        num_scalar_prefetch=2,
        grid=(n_blocks,),
        in_specs=[
            pl.BlockSpec((MOE_TM,) + ROW_TILE, blk),
            pl.BlockSpec((None, None, D_MODEL, D_FF_EXPERT), wmap),
            pl.BlockSpec((None, None, D_MODEL, D_FF_EXPERT), wmap),
            pl.BlockSpec((None, None, D_FF_EXPERT, D_MODEL), wmap),
        ],
        out_specs=pl.BlockSpec((MOE_TM,) + ROW_TILE, lambda i, be, nu: (i, 0, 0)),
    )
    return pl.pallas_call(
        _expert_kernel,
        out_shape=jax.ShapeDtypeStruct((n_slots,) + ROW_TILE, BF16),
        grid_spec=grid_spec,
        compiler_params=_cparams(("arbitrary",), 48),
        name="moe_experts",
    )(block_expert, n_used, xs, wg, wu, wd)


def _combine_kernel(id0_ref, id1_ref, idn_ref, x1_ref, gate_ref, ys_hbm, nw_ref, out_ref, idx_smem, buf_a, buf_b,
                    sem, isem, *, final_norm):
    i = pl.program_id(0)
    n = pl.num_programs(0)
    slot = i % 2

    def row(tile_slot, r, k, src_row):
        buf = buf_a if k == 0 else buf_b
        return pltpu.make_async_copy(ys_hbm.at[pl.ds(src_row, 1)], buf.at[tile_slot, pl.ds(r, 1)],
                                     sem.at[tile_slot])

    def issue_rows(tile_slot):
        def body(r, carry):
            row(tile_slot, r, 0, idx_smem[tile_slot, r]).start(priority=0)
            row(tile_slot, r, 1, idx_smem[tile_slot, ROW_TM + r]).start(priority=1)
            return carry
        lax.fori_loop(0, ROW_TM, body, 0, unroll=8)

    @pl.when(i == 0)
    def _():
        cp = _idx_copy(id0_ref, idx_smem, 0, isem)
        cp.start()
        cp.wait()
        issue_rows(0)

        @pl.when(n > 1)
        def _():
            cp1 = _idx_copy(id1_ref, idx_smem, 1, isem)
            cp1.start()
            cp1.wait()

    @pl.when(i + 1 < n)
    def _():
        issue_rows(1 - slot)

    @pl.when(i + 2 < n)
    def _():
        _idx_copy(idn_ref, idx_smem, slot, isem).start()

    def drain(r, carry):
        row(slot, r, 0, 0).wait()
        row(slot, r, 1, 0).wait()
        return carry

    lax.fori_loop(0, ROW_TM, drain, 0, unroll=8)

    @pl.when(i + 2 < n)
    def _():
        _idx_copy(idn_ref, idx_smem, slot, isem).wait()

    a = _tiles_to_rows(buf_a[slot]).astype(F32)
    b = _tiles_to_rows(buf_b[slot]).astype(F32)
    o = x1_ref[...] + (a * gate_ref[:, 0:1] + b * gate_ref[:, 1:2])
    if final_norm:
        o = _rms(o, nw_ref[...])
    out_ref[...] = o


def _combine(x1, gate, ys, dest_idx, nw, *, tile0, n_tiles, final_norm):
    last = tile0 + n_tiles - 1
    idx_block = (1, 1, TOP_K * ROW_TM)
    return pl.pallas_call(
        functools.partial(_combine_kernel, final_norm=final_norm),
        out_shape=jax.ShapeDtypeStruct((n_tiles * ROW_TM, D_MODEL), F32),
        grid=(n_tiles,),
        in_specs=[pl.BlockSpec(idx_block, lambda i: (tile0, 0, 0)),
                  pl.BlockSpec(idx_block, lambda i: (min(tile0 + 1, last), 0, 0)),
                  pl.BlockSpec(idx_block, lambda i: (jnp.minimum(tile0 + i + 2, last), 0, 0)),
                  pl.BlockSpec((ROW_TM, D_MODEL), lambda i: (tile0 + i, 0)),
                  pl.BlockSpec((ROW_TM, LANES), lambda i: (tile0 + i, 0)),
                  pl.BlockSpec(memory_space=pl.ANY),
                  pl.BlockSpec((1, D_MODEL), lambda i: (0, 0))],
        out_specs=pl.BlockSpec((ROW_TM, D_MODEL), lambda i: (i, 0)),
        scratch_shapes=[pltpu.SMEM((2, TOP_K * ROW_TM), I32), pltpu.VMEM((2, ROW_TM) + ROW_TILE, BF16),
                        pltpu.VMEM((2, ROW_TM) + ROW_TILE, BF16), pltpu.SemaphoreType.DMA((2,)),
                        pltpu.SemaphoreType.DMA],
        compiler_params=_cparams(("arbitrary",), 24),
        name="moe_combine",
    )(dest_idx, dest_idx, dest_idx, x1, gate, ys, nw)


def _dispatch_plan(eid, rank, counts):
    t = eid.shape[0]
    n_assign = t * TOP_K
    padded = (counts + MOE_TM - 1) // MOE_TM * MOE_TM
    pend = jnp.cumsum(padded)
    pstart = pend - padded
    dest = pstart[eid] + rank
    n_blocks = -(-n_assign // MOE_TM) + N_EXPERTS
    n_slots = n_blocks * MOE_TM
    block_start = jnp.arange(n_blocks, dtype=I32) * MOE_TM
    block_expert = jnp.minimum(jnp.sum((pend[None, :] <= block_start[:, None]).astype(I32), axis=1),
                               N_EXPERTS - 1).astype(I32)
    n_used = (pend[-1] // MOE_TM).astype(I32).reshape(1)
    zstart = jnp.minimum((pstart + counts) // SUBLANES * SUBLANES, n_slots - MOE_TM).astype(I32)
    dest_idx = dest.reshape(t // ROW_TM, ROW_TM, TOP_K).transpose(0, 2, 1).reshape(t // ROW_TM, 1, TOP_K * ROW_TM)
    return dest_idx.astype(I32), block_expert, n_used, zstart, n_slots


def _seq_flags(seq_lens, tile):
    first, last = [], []
    for s in seq_lens:
        assert s % tile == 0
        n = s // tile
        first += [1] + [0] * (n - 1)
        last += [0] * (n - 1) + [1]
    return jnp.asarray(first, I32), jnp.asarray(last, I32)


def _rope_tables(seq_lens):
    pos = jnp.concatenate([jnp.arange(s, dtype=F32) for s in seq_lens])
    inv_freq = ROPE_THETA ** (-jnp.arange(0, ROPE_DIM, 2, dtype=F32) / ROPE_DIM)
    ang = pos[:, None] * inv_freq[None, :]
    d = jnp.arange(LANES) % HEAD_DIM
    cos = jnp.cos(ang)[:, d % ROPE_HALF]
    sin = jnp.sin(ang)[:, d % ROPE_HALF]
    c = jnp.where(d < ROPE_DIM, cos, 1.0)
    sa = jnp.where((d >= ROPE_HALF) & (d < ROPE_DIM), sin, 0.0)
    sb = jnp.where(d < ROPE_HALF, -sin, 0.0)
    return c, sa, sb


def _head_expand(d0):
    rows = jnp.arange(LANES)[:, None]
    cols = jnp.arange(SSM_WIDTH)[None, :] // SSM_HEAD_DIM
    return (rows == cols + d0).astype(BF16)


def _pad_lanes(v, width=LANES):
    v = v.reshape(1, -1)
    return jnp.pad(v, ((0, 0), (0, width - v.shape[1])))


def _trunk(xs, seq_lens, attn_norm_w, w_in, conv_w, conv_b, attn_sink, attn_out_norm_w, ssm_a_log, ssm_dt_bias,
           ssm_d, ssm_norm_w, w_out, ffn_norm_w, w_router_group, b_router_group, w_router_expert,
           b_router_expert, w_gate, w_up, w_down, final_norm_w, out_tiles):
    depth = w_in.shape[0]
    t = sum(seq_lens)
    rope_c, rope_sa, rope_sb = _rope_tables(seq_lens)
    tile_first, tile_last = _seq_flags(seq_lens, ATTN_TQ)
    chunk_first, chunk_last = _seq_flags(seq_lens, CHUNK)
    e_fwd, e_bwd = _head_expand(0), _head_expand(SSM_HEADS)
    shift = _conv_shift_matrix()
    tri = (jnp.arange(ROUTE_TM)[:, None] > jnp.arange(ROUTE_TM)[None, :]).astype(BF16)
    z_end = ATTN_WIDTH + 2 * KV_WIDTH + SSM_WIDTH
    w_out_b, w_gate_b, w_up_b, w_down_b = (v.astype(BF16) for v in (w_out, w_gate, w_up, w_down))
    outs = None
    for l in range(depth):
        w = w_in[l]
        w_main = jnp.concatenate([w[:, :ATTN_WIDTH + 2 * KV_WIDTH], w[:, z_end:z_end + CONV_CH],
                                  w[:, ATTN_WIDTH + 2 * KV_WIDTH:z_end]], axis=1).astype(BF16)
        w_dt = jnp.pad(w[:, z_end + CONV_CH:], ((0, 0), (0, LANES - 2 * SSM_HEADS))).astype(BF16)
        proj, dt_raw = _proj(xs, attn_norm_w[l].reshape(1, -1), w_main, w_dt, rope_c, rope_sa, rope_sb)
        attn = _attention(proj, tile_first, tile_last, attn_sink[l] * LOG2E, attn_out_norm_w[l].reshape(1, -1))
        cw = jnp.pad(conv_w[l], ((0, SUBLANES - CONV_WIDTH), (0, 0)))
        cb = conv_b[l].reshape(1, -1)
        dtb = _pad_lanes(ssm_dt_bias[l])
        alog = _pad_lanes(ssm_a_log[l])
        dexp = jnp.repeat(ssm_d[l], SSM_HEAD_DIM).reshape(1, -1)
        y_fwd, u_fwd = _ssd(proj, dt_raw, chunk_first, chunk_last, dtb, alog, e_fwd, dexp, backward=False,
                            conv=(shift, cw, cb))
        ssd = _ssd(proj, dt_raw, chunk_first, chunk_last, dtb, alog, e_bwd, ssm_norm_w[l].reshape(1, -1),
                   backward=True, u_fwd=u_fwd, y_fwd=y_fwd)
        wr = jnp.pad(jnp.concatenate([w_router_group[l], w_router_expert[l]], axis=1),
                     ((0, 0), (0, LANES - N_EXPERT_GROUPS - N_EXPERTS)))
        wr_hi, wr_lo = _split2(wr)
        br = _pad_lanes(jnp.concatenate([b_router_group[l], b_router_expert[l]]))
        x1, h2p, logits = _outproj(attn, ssd, xs, w_out_b, l, ffn_norm_w[l].reshape(1, -1),
                                   jnp.concatenate([wr_hi, wr_lo], axis=1), wr_hi, br)
        eid, gate, cnt = _route(logits, tri)
        counts = cnt[0, :N_EXPERTS].astype(I32)
        dest_idx, block_expert, n_used, zstart, n_slots = _dispatch_plan(eid[:, :TOP_K], eid[:, TOP_K:2 * TOP_K],
                                                                        counts)
        xslots = _dispatch(h2p, dest_idx, zstart, n_used, n_slots)
        ys = _experts(xslots, block_expert, n_used, w_gate_b, w_up_b, w_down_b, l)
        nw = final_norm_w.reshape(1, -1)
        if l == depth - 1:
            outs, tile0 = [], 0
            for nt in out_tiles:
                outs.append(_combine(x1, gate, ys, dest_idx, nw, tile0=tile0, n_tiles=nt, final_norm=True))
                tile0 += nt
        else:
            xs = [_combine(x1, gate, ys, dest_idx, nw, tile0=0, n_tiles=t // ROW_TM, final_norm=False)]
    return outs


def kernel(x_prompt, x_sample, attn_norm_w, w_in, conv_w, conv_b, attn_sink, attn_out_norm_w, ssm_a_log,
           ssm_dt_bias, ssm_d, ssm_norm_w, w_out, ffn_norm_w, w_router_group, b_router_group, w_router_expert,
           b_router_expert, w_gate, w_up, w_down, final_norm_w):
    bp, sp, d = x_prompt.shape
    bs, ss, _ = x_sample.shape
    seq_lens = [sp] * bp + [ss] * bs
    xs = [x_prompt.reshape(bp * sp, d), x_sample.reshape(bs * ss, d)]
    y_prompt, y_sample = _trunk(xs, seq_lens, attn_norm_w, w_in, conv_w, conv_b, attn_sink, attn_out_norm_w,
                                ssm_a_log, ssm_dt_bias, ssm_d, ssm_norm_w, w_out, ffn_norm_w, w_router_group,
                                b_router_group, w_router_expert, b_router_expert, w_gate, w_up, w_down,
                                final_norm_w, out_tiles=[bp * sp // ROW_TM, bs * ss // ROW_TM])
    return (y_prompt.reshape(bp, sp, d), y_sample.reshape(bs, ss, d))
```

```python
import functools
import math

import numpy as np
import jax
import jax.numpy as jnp
from jax import lax
from jax.experimental import pallas as pl
from jax.experimental.pallas import tpu as pltpu

D_MODEL = 2048
N_Q_HEADS = 16
N_KV_HEADS = 4
HEAD_DIM = 64
Q_GROUP = N_Q_HEADS // N_KV_HEADS
ATTN_WIDTH = N_Q_HEADS * HEAD_DIM
KV_WIDTH = N_KV_HEADS * HEAD_DIM
WINDOW = 128
BLOCK = 128
ROPE_THETA = 500000.0
ROPE_DIM = HEAD_DIM // 4
ROPE_HALF = ROPE_DIM // 2
SSM_HEADS = 16
SSM_HEAD_DIM = 64
SSM_WIDTH = SSM_HEADS * SSM_HEAD_DIM
SSM_GROUPS = 2
HEADS_PER_GROUP = SSM_HEADS // SSM_GROUPS
D_STATE = 128
BC_WIDTH = SSM_GROUPS * D_STATE
CONV_WIDTH = 5
CONV_PAD = CONV_WIDTH // 2
CONV_CH = SSM_WIDTH + 2 * BC_WIDTH
CHUNK = 128
N_EXPERT_GROUPS = 4
EXPERTS_PER_GROUP = 8
N_EXPERTS = N_EXPERT_GROUPS * EXPERTS_PER_GROUP
TOP_K = 2
D_FF_EXPERT = 1024
EPS = 1e-6

LANES = 128
SUBLANES = 8
BF16_SUBLANES = 16
MIB = 1024 * 1024

COL_Q = 0
COL_K = ATTN_WIDTH
COL_V = COL_K + KV_WIDTH
COL_XBC = COL_V + KV_WIDTH
COL_Z = COL_XBC + CONV_CH
PROJ_MAIN = COL_Z + SSM_WIDTH
PROJ_TM = 512
PROJ_TN = 1024
ATTN_TQ = 512
OUT_TM = 256
ROUTE_TM = 512
SSD_NCH = 4
MOE_TM = 256
ROW_TM = 256
CONV_K = 2 * CHUNK
NEG = -1e30
LOG2E = math.log2(math.e)
Q_SCALE = HEAD_DIM ** -0.5 * LOG2E
F32 = jnp.float32
BF16 = jnp.bfloat16
I32 = jnp.int32


def _cparams(semantics, vmem_mib):
    return pltpu.CompilerParams(dimension_semantics=semantics, vmem_limit_bytes=vmem_mib * MIB)


def _rms(x, w):
    ms = jnp.mean(x * x, axis=-1, keepdims=True)
    return (x * lax.rsqrt(ms + EPS)) * w


def _silu(x):
    return x / (1.0 + jnp.exp(-x))


def _split2(v):
    hi = v.astype(BF16)
    lo = (v - hi.astype(F32)).astype(BF16)
    return hi, lo


def _split3(v):
    hi = v.astype(BF16)
    r = v - hi.astype(F32)
    mid = r.astype(BF16)
    lo = (r - mid.astype(F32)).astype(BF16)
    return hi, mid, lo


ROW_TILE = (BF16_SUBLANES, D_MODEL // BF16_SUBLANES)


def _rows_to_tiles(v):
    return v.astype(BF16).reshape((v.shape[0],) + ROW_TILE)


def _tiles_to_rows(v):
    return v.reshape(v.shape[0], D_MODEL)


def _select_src(i, x_refs, src_tiles):
    if len(x_refs) == 1:
        return x_refs[0][...]
    return jnp.where(i < src_tiles, x_refs[0][...], x_refs[1][...])


def _src_specs(xs, tm, width, nargs):
    if len(xs) == 1:
        return [pl.BlockSpec((tm, width), lambda i, *_: (i, 0))], 0
    n0 = xs[0].shape[0] // tm
    return [pl.BlockSpec((tm, width), lambda i, *_: (jnp.minimum(i, n0 - 1), 0)),
            pl.BlockSpec((tm, width), lambda i, *_: (jnp.maximum(i - n0, 0), 0))], n0


def _proj_kernel(*refs, n_src, src_tiles):
    x_refs = refs[:n_src]
    nw_ref, w_ref, wdt_ref, c_ref, sa_ref, sb_ref, out_ref, dt_ref = refs[n_src:]
    i = pl.program_id(0)
    h = _rms(_select_src(i, x_refs, src_tiles), nw_ref[...]).astype(BF16)
    dt_ref[...] = jnp.dot(h, wdt_ref[...], preferred_element_type=F32)

    def rope(a):
        return (a * c_ref[...] + pltpu.roll(a, ROPE_HALF, 1) * sa_ref[...]
                + pltpu.roll(a, LANES - ROPE_HALF, 1) * sb_ref[...])

    for jt in range(PROJ_MAIN // PROJ_TN):
        c0 = jt * PROJ_TN
        acc = jnp.dot(h, w_ref[:, c0:c0 + PROJ_TN], preferred_element_type=F32)
        for cb in range(PROJ_TN // LANES):
            col = c0 + cb * LANES
            a = acc[:, cb * LANES:(cb + 1) * LANES]
            if col < COL_V:
                a = rope(a)
            if col < COL_K:
                a = a * Q_SCALE
            out_ref[:, col:col + LANES] = a.astype(BF16)


def _proj(xs, nw, w_main, w_dt, rope_c, rope_sa, rope_sb):
    t = sum(x.shape[0] for x in xs)
    assert t % PROJ_TM == 0 and all(x.shape[0] % PROJ_TM == 0 for x in xs)
    row = lambda i: (i, 0)
    const = lambda i: (0, 0)
    once = dict(pipeline_mode=pl.Buffered(1))
    src_specs, n0 = _src_specs(xs, PROJ_TM, D_MODEL, 1)
    return pl.pallas_call(
        functools.partial(_proj_kernel, n_src=len(xs), src_tiles=n0),
        out_shape=(jax.ShapeDtypeStruct((t, PROJ_MAIN), BF16), jax.ShapeDtypeStruct((t, LANES), F32)),
        grid=(t // PROJ_TM,),
        in_specs=src_specs + [
            pl.BlockSpec((1, D_MODEL), const),
            pl.BlockSpec((D_MODEL, PROJ_MAIN), const, **once),
            pl.BlockSpec((D_MODEL, LANES), const, **once),
            pl.BlockSpec((PROJ_TM, LANES), row),
            pl.BlockSpec((PROJ_TM, LANES), row),
            pl.BlockSpec((PROJ_TM, LANES), row),
        ],
        out_specs=(pl.BlockSpec((PROJ_TM, PROJ_MAIN), row), pl.BlockSpec((PROJ_TM, LANES), row)),
        compiler_params=_cparams(("arbitrary",), 56),
        name="proj",
    )(*xs, nw, w_main, w_dt, rope_c, rope_sa, rope_sb)


def _attn_kernel(tf_ref, tl_ref, sink_ref, q_ref, kc_ref, vc_ref, kp_ref, vp_ref, kn_ref, vn_ref,
                 nw_ref, o_ref):
    i = pl.program_id(0)
    kw = jnp.concatenate([kp_ref[...], kc_ref[...], kn_ref[...]], axis=0)
    vw = jnp.concatenate([vp_ref[...], vc_ref[...], vn_ref[...]], axis=0)
    rows = kw.shape[0]
    lane_w = lax.broadcasted_iota(I32, (rows, LANES), 1)
    low = lane_w < HEAD_DIM
    def pad_pair(slab, g):
        s = slab.astype(F32)
        if g % 2 == 0:
            e = jnp.where(low, s, 0.0)
            o = pltpu.roll(e, HEAD_DIM, 1)
        else:
            o = jnp.where(low, 0.0, s)
            e = pltpu.roll(o, HEAD_DIM, 1)
        return e, o

    ke, ko, vx = [], [], []
    for g in range(N_KV_HEADS):
        c0 = (g // 2) * LANES
        e, o = pad_pair(kw[:, c0:c0 + LANES], g)
        ke.append(e.astype(BF16))
        ko.append(o.astype(BF16))
        e, _ = pad_pair(vw[:, c0:c0 + LANES], g)
        vx.append(jnp.where(lane_w == HEAD_DIM, 1.0, e).astype(BF16))

    qi = lax.broadcasted_iota(I32, (BLOCK, BLOCK), 0)
    kk = lax.broadcasted_iota(I32, (BLOCK, BLOCK), 1)
    bias_prev = jnp.where(kk >= qi, 0.0, NEG)
    bias_next = jnp.where(kk <= qi, 0.0, NEG)
    lane_b = lax.broadcasted_iota(I32, (BLOCK, LANES), 1)
    nsb = ATTN_TQ // BLOCK
    for sb in range(nsb):
        bp, bn = bias_prev, bias_next
        if sb == 0:
            bp = jnp.where(tf_ref[i] == 1, NEG, bias_prev)
        if sb == nsb - 1:
            bn = jnp.where(tl_ref[i] == 1, NEG, bias_next)
        r0 = sb * BLOCK
        slabs = []
        for g in range(N_KV_HEADS):
            q0 = g * Q_GROUP * HEAD_DIM
            lhs = jnp.concatenate([q_ref[r0:r0 + BLOCK, q0:q0 + LANES],
                                   q_ref[r0:r0 + BLOCK, q0 + LANES:q0 + 2 * LANES]], axis=0)
            rhs = jnp.concatenate([ke[g][r0:r0 + 3 * BLOCK], ko[g][r0:r0 + 3 * BLOCK]], axis=0)
            s = lax.dot_general(lhs, rhs, (((1,), (1,)), ((), ())), preferred_element_type=F32)
            ps, ms = [], []
            for hi in range(Q_GROUP):
                c0 = (hi % 2) * 3 * BLOCK
                sq = s[(hi // 2) * BLOCK:(hi // 2 + 1) * BLOCK, c0:c0 + 3 * BLOCK]
                s0 = sq[:, :BLOCK] + bp
                s1 = sq[:, BLOCK:2 * BLOCK]
                s2 = sq[:, 2 * BLOCK:] + bn
                m = jnp.max(jnp.maximum(jnp.maximum(s0, s1), s2), axis=-1, keepdims=True)
                m = jnp.maximum(m, sink_ref[g * Q_GROUP + hi])
                ps.append(jnp.concatenate([jnp.exp2(s0 - m).astype(BF16), jnp.exp2(s1 - m).astype(BF16),
                                           jnp.exp2(s2 - m).astype(BF16)], axis=1))
                ms.append(m)
            o4 = jnp.dot(jnp.concatenate(ps, axis=0), vx[g][r0:r0 + 3 * BLOCK], preferred_element_type=F32)
            normed = []
            for hi in range(Q_GROUP):
                oh = o4[hi * BLOCK:(hi + 1) * BLOCK]
                den = oh[:, HEAD_DIM:HEAD_DIM + 1] + jnp.exp2(sink_ref[g * Q_GROUP + hi] - ms[hi])
                normed.append(oh * (1.0 / den))
            for half in range(2):
                slabs.append(jnp.where(lane_b < HEAD_DIM, normed[2 * half],
                                       pltpu.roll(normed[2 * half + 1], HEAD_DIM, 1)))
        o = jnp.concatenate(slabs, axis=1)
        o_ref[r0:r0 + BLOCK, :] = _rms(o, nw_ref[...]).astype(BF16)


def _attention(proj, tile_first, tile_last, sink2, nw):
    t = proj.shape[0]
    assert t % ATTN_TQ == 0
    nb = t // BLOCK
    r = ATTN_TQ // BLOCK
    kcol, vcol = COL_K // KV_WIDTH, COL_V // KV_WIDTH
    prev = lambda i: jnp.maximum(i * r - 1, 0)
    nxt = lambda i: jnp.minimum(i * r + r, nb - 1)
    grid_spec = pltpu.PrefetchScalarGridSpec(
        num_scalar_prefetch=3,
        grid=(t // ATTN_TQ,),
        in_specs=[
            pl.BlockSpec((ATTN_TQ, ATTN_WIDTH), lambda i, *_: (i, COL_Q // ATTN_WIDTH)),
            pl.BlockSpec((ATTN_TQ, KV_WIDTH), lambda i, *_: (i, kcol)),
            pl.BlockSpec((ATTN_TQ, KV_WIDTH), lambda i, *_: (i, vcol)),
            pl.BlockSpec((BLOCK, KV_WIDTH), lambda i, *_: (prev(i), kcol)),
            pl.BlockSpec((BLOCK, KV_WIDTH), lambda i, *_: (prev(i), vcol)),
            pl.BlockSpec((BLOCK, KV_WIDTH), lambda i, *_: (nxt(i), kcol)),
            pl.BlockSpec((BLOCK, KV_WIDTH), lambda i, *_: (nxt(i), vcol)),
            pl.BlockSpec((1, ATTN_WIDTH), lambda i, *_: (0, 0)),
        ],
        out_specs=pl.BlockSpec((ATTN_TQ, ATTN_WIDTH), lambda i, *_: (i, 0)),
    )
    return pl.pallas_call(
        _attn_kernel,
        out_shape=jax.ShapeDtypeStruct((t, ATTN_WIDTH), BF16),
        grid_spec=grid_spec,
        compiler_params=_cparams(("arbitrary",), 32),
        name="attn",
    )(tile_first, tile_last, sink2, proj, proj, proj, proj, proj, proj, proj, nw)


def _ssd_kernel(first_ref, last_ref, *refs, backward):
    if backward:
        xc_ref, dt_ref, *rest = refs
        xp_ref = xn_ref = None
    else:
        xc_ref, xp_ref, xn_ref, dt_ref, *rest = refs
    h_ref = rest[-1]
    i = pl.program_id(0)
    tile = (pl.num_programs(0) - 1 - i) if backward else i
    order = range(SSD_NCH - 1, -1, -1) if backward else range(SSD_NCH)

    @pl.when(i == 0)
    def _():
        h_ref[...] = jnp.zeros_like(h_ref)

    h = h_ref[...]
    for k in order:
        h = _ssd_chunk(k, tile * SSD_NCH + k, h, first_ref, last_ref, xc_ref, xp_ref, xn_ref, dt_ref, rest,
                       backward)
    h_ref[...] = h


def _ssd_chunk(k, c, hprev, first_ref, last_ref, xc_ref, xp_ref, xn_ref, dt_ref, rest, backward):
    is_first = first_ref[c] == 1
    is_last = last_ref[c] == 1
    hprev = jnp.where(is_last if backward else is_first, jnp.zeros_like(hprev), hprev)
    r0 = k * CHUNK
    rows = pl.ds(r0, CHUNK)

    if backward:
        (z_ref, yf_ref, dtb_ref, alog_ref, e_ref, nw_ref, out_ref, _) = rest
        u = xc_ref[rows, :].astype(F32)
    else:
        (sh_ref, cw_ref, cb_ref, dtb_ref, alog_ref, e_ref, dexp_ref, out_ref, u_ref, _) = rest
        halo = BF16_SUBLANES
        xc = xc_ref[rows, :]
        no_rows = jnp.zeros((halo, CONV_CH), BF16)
        before = xp_ref[...] if k == 0 else xc_ref[r0 - halo:r0, :]
        after = xn_ref[...] if k == SSD_NCH - 1 else xc_ref[r0 + CHUNK:r0 + CHUNK + halo, :]
        xp = jnp.where(is_first, no_rows, before)
        xn = jnp.where(is_last, no_rows, after)
        xw = jnp.concatenate([xc, xp, xn, jnp.zeros((CONV_K - CHUNK - 2 * halo, CONV_CH), BF16)], axis=0)
        shifted = jnp.dot(sh_ref[...], xw, preferred_element_type=F32)
        acc = cb_ref[...] + xc.astype(F32) * cw_ref[CONV_PAD:CONV_PAD + 1, :]
        taps = [tap for tap in range(CONV_WIDTH) if tap != CONV_PAD]
        for n, tap in enumerate(taps):
            acc = acc + shifted[n * CHUNK:(n + 1) * CHUNK] * cw_ref[tap:tap + 1, :]
        u = _silu(acc)
        u_ref[rows, :] = u.astype(BF16)
    xs = u[:, :SSM_WIDTH]
    bm = u[:, SSM_WIDTH:SSM_WIDTH + BC_WIDTH]
    cm = u[:, SSM_WIDTH + BC_WIDTH:]

    d0 = SSM_HEADS if backward else 0
    draw = dt_ref[rows, :] + dtb_ref[...]
    dt = jnp.maximum(draw, 0.0) + jnp.log1p(jnp.exp(-jnp.abs(draw)))
    a = dt * (-jnp.exp(alog_ref[...]))
    li = lax.broadcasted_iota(I32, (CHUNK, CHUNK), 0)
    ti = lax.broadcasted_iota(I32, (CHUNK, CHUNK), 1)
    causal = (li <= ti) if backward else (li >= ti)
    tri = jnp.where(causal, 1.0, 0.0).astype(BF16)
    a_hi, a_mid, a_lo = _split3(a)
    acum = (jnp.dot(tri, a_hi, preferred_element_type=F32) + jnp.dot(tri, a_mid, preferred_element_type=F32)
            + jnp.dot(tri, a_lo, preferred_element_type=F32))
    end = 0 if backward else CHUNK - 1
    tot = acum[end:end + 1, :]
    exp_a = jnp.exp(acum)
    wdec = dt * jnp.exp(tot - acum)
    stack = jnp.concatenate(_split2(exp_a) + (wdec.astype(BF16),), axis=0)
    ex = jnp.dot(stack, e_ref[...], preferred_element_type=F32)
    ea_e = ex[0:CHUNK] + ex[CHUNK:2 * CHUNK]
    wd_e = ex[2 * CHUNK:3 * CHUNK]
    xd = (xs * wd_e).astype(BF16)
    acum_t = (acum - jnp.log(dt)).T
    hb = hprev.astype(BF16)
    lane = lax.broadcasted_iota(I32, (CHUNK, LANES), 1)
    gw = HEADS_PER_GROUP * SSM_HEAD_DIM
    ys = []
    sts = []
    for g in range(SSM_GROUPS):
        bg = bm[:, g * D_STATE:(g + 1) * D_STATE]
        cg = cm[:, g * D_STATE:(g + 1) * D_STATE].astype(BF16)
        cb = lax.dot_general(cg, bg.astype(BF16), (((1,), (1,)), ((), ())), preferred_element_type=F32)
        yoff = jnp.dot(cg, hb[:, g * gw:(g + 1) * gw], preferred_element_type=F32)
        sts.append(jnp.dot(bg.T.astype(BF16), xd[:, g * gw:(g + 1) * gw], preferred_element_type=F32))
        for pr in range(HEADS_PER_GROUP // 2):
            h0 = g * HEADS_PER_GROUP + 2 * pr
            ms = []
            for hh in (h0, h0 + 1):
                col = acum[:, d0 + hh:d0 + hh + 1]
                row = acum_t[d0 + hh:d0 + hh + 1, :]
                decay = jnp.exp(jnp.where(causal, col - row, NEG))
                ms.append((cb * decay).astype(BF16))
            lhs = jnp.concatenate(ms, axis=1)
            c0 = (h0 // 2) * LANES
            slab = xs[:, c0:c0 + LANES]
            rhs = jnp.concatenate([jnp.where(lane < SSM_HEAD_DIM, slab, 0.0),
                                   jnp.where(lane >= SSM_HEAD_DIM, slab, 0.0)], axis=0).astype(BF16)
            yd = jnp.dot(lhs, rhs, preferred_element_type=F32)
            ys.append(yd + yoff[:, pr * LANES:(pr + 1) * LANES] * ea_e[:, c0:c0 + LANES])
    y = jnp.concatenate(ys, axis=1)
    if backward:
        yt = yf_ref[rows, :] + y
        gz = yt * _silu(z_ref[rows, :].astype(F32))
        out_ref[rows, :] = _rms(gz, nw_ref[...]).astype(BF16)
    else:
        out_ref[rows, :] = y + xs * dexp_ref[...]
    return hprev * ea_e[end:end + 1, :] + jnp.concatenate(sts, axis=1)


def _conv_shift_matrix():
    halo = BF16_SUBLANES
    taps = [k for k in range(CONV_WIDTH) if k != CONV_PAD]
    m = np.zeros((len(taps) * CHUNK, CONV_K), np.float32)
    for n, k in enumerate(taps):
        for t in range(CHUNK):
            j = t + k - CONV_PAD
            if j < 0:
                col = CHUNK + halo + j
            elif j >= CHUNK:
                col = CHUNK + halo + (j - CHUNK)
            else:
                col = j
            m[n * CHUNK + t, col] = 1.0
    return jnp.asarray(m, BF16)


def _ssd(proj, dt_raw, chunk_first, chunk_last, dt_bias, a_log, expand, extra, *, backward, conv=None,
         u_fwd=None, y_fwd=None):
    t = proj.shape[0]
    tm = SSD_NCH * CHUNK
    assert t % tm == 0
    nc = t // tm
    hb = tm // BF16_SUBLANES
    n_halo = t // BF16_SUBLANES
    xcol = COL_XBC // CONV_CH
    ch = (lambda i: nc - 1 - i) if backward else (lambda i: i)
    const = lambda i, *_: (0, 0)
    tile = lambda width, col=0: pl.BlockSpec((tm, width), lambda i, *_: (ch(i), col))
    small = [pl.BlockSpec((1, LANES), const), pl.BlockSpec((1, LANES), const),
             pl.BlockSpec((LANES, SSM_WIDTH), const), pl.BlockSpec((1, SSM_WIDTH), const)]
    if backward:
        in_specs = [tile(CONV_CH), tile(LANES), tile(SSM_WIDTH, COL_Z // SSM_WIDTH), tile(SSM_WIDTH)] + small
        args = [u_fwd, dt_raw, proj, y_fwd, dt_bias, a_log, expand, extra]
        out_shape = jax.ShapeDtypeStruct((t, SSM_WIDTH), BF16)
        out_specs = tile(SSM_WIDTH)
    else:
        shift, conv_w, conv_b = conv
        in_specs = [
            tile(CONV_CH, xcol),
            pl.BlockSpec((BF16_SUBLANES, CONV_CH), lambda i, *_: (jnp.maximum(ch(i) * hb - 1, 0), xcol)),
            pl.BlockSpec((BF16_SUBLANES, CONV_CH), lambda i, *_: (jnp.minimum(ch(i) * hb + hb, n_halo - 1), xcol)),
            tile(LANES),
            pl.BlockSpec(shift.shape, const),
            pl.BlockSpec((SUBLANES, CONV_CH), const),
            pl.BlockSpec((1, CONV_CH), const),
        ] + small
        args = [proj, proj, proj, dt_raw, shift, conv_w, conv_b, dt_bias, a_log, expand, extra]
        out_shape = (jax.ShapeDtypeStruct((t, SSM_WIDTH), F32), jax.ShapeDtypeStruct((t, CONV_CH), BF16))
        out_specs = (tile(SSM_WIDTH), tile(CONV_CH))
    grid_spec = pltpu.PrefetchScalarGridSpec(
        num_scalar_prefetch=2,
        grid=(nc,),
        in_specs=in_specs,
        out_specs=out_specs,
        scratch_shapes=[pltpu.VMEM((D_STATE, SSM_WIDTH), F32)],
    )
    return pl.pallas_call(
        functools.partial(_ssd_kernel, backward=backward),
        out_shape=out_shape,
        grid_spec=grid_spec,
        compiler_params=_cparams(("arbitrary",), 32),
        name="ssd_bwd" if backward else "ssd_fwd",
    )(chunk_first, chunk_last, *args)


def _outproj_kernel(*refs, n_src, src_tiles):
    a_ref, s_ref = refs[:2]
    x_refs = refs[2:2 + n_src]
    wa_ref, ws_ref, nw_ref, wr1_ref, wr2_ref, br_ref, x1_ref, h2_ref, lg_ref = refs[2 + n_src:]
    i = pl.program_id(0)
    y = (jnp.dot(a_ref[...], wa_ref[...], preferred_element_type=F32)
         + jnp.dot(s_ref[...], ws_ref[...], preferred_element_type=F32))
    x1 = _select_src(i, x_refs, src_tiles) + y
    x1_ref[...] = x1
    h2 = _rms(x1, nw_ref[...])
    h2_ref[...] = _rows_to_tiles(h2)
    h_hi, h_lo = _split2(h2)
    part = jnp.dot(h_hi, wr1_ref[...], preferred_element_type=F32)
    lg_ref[...] = (part[:, :LANES] + (part[:, LANES:] + jnp.dot(h_lo, wr2_ref[...], preferred_element_type=F32))
                   + br_ref[...])


def _outproj(attn, ssd, xs, w_out, layer, nw, wr1, wr2, br):
    t = attn.shape[0]
    assert t % OUT_TM == 0
    row = lambda i: (i, 0)
    const = lambda i: (0, 0)
    once = dict(pipeline_mode=pl.Buffered(1))
    src_specs, n0 = _src_specs(xs, OUT_TM, D_MODEL, 1)
    return pl.pallas_call(
        functools.partial(_outproj_kernel, n_src=len(xs), src_tiles=n0),
        out_shape=(jax.ShapeDtypeStruct((t, D_MODEL), F32), jax.ShapeDtypeStruct((t,) + ROW_TILE, BF16),
                   jax.ShapeDtypeStruct((t, LANES), F32)),
        grid=(t // OUT_TM,),
        in_specs=[pl.BlockSpec((OUT_TM, ATTN_WIDTH), row), pl.BlockSpec((OUT_TM, SSM_WIDTH), row)] + src_specs + [
            pl.BlockSpec((None, ATTN_WIDTH, D_MODEL), lambda i: (layer, 0, 0), **once),
            pl.BlockSpec((None, SSM_WIDTH, D_MODEL), lambda i: (layer, ATTN_WIDTH // SSM_WIDTH, 0), **once),
            pl.BlockSpec((1, D_MODEL), const),
            pl.BlockSpec((D_MODEL, 2 * LANES), const, **once),
            pl.BlockSpec((D_MODEL, LANES), const, **once),
            pl.BlockSpec((1, LANES), const),
        ],
        out_specs=(pl.BlockSpec((OUT_TM, D_MODEL), row), pl.BlockSpec((OUT_TM,) + ROW_TILE, lambda i: (i, 0, 0)),
                   pl.BlockSpec((OUT_TM, LANES), row)),
        compiler_params=_cparams(("arbitrary",), 44),
        name="outproj",
    )(attn, ssd, *xs, w_out, w_out, nw, wr1, wr2, br)


def _route_kernel(lg_ref, tri_ref, eid_ref, gate_ref, cnt_ref, run_ref):
    i = pl.program_id(0)

    @pl.when(i == 0)
    def _():
        run_ref[...] = jnp.zeros_like(run_ref)

    logits = lg_ref[...]
    lane = lax.broadcasted_iota(I32, logits.shape, 1)
    big = jnp.int32(LANES)
    gl = jnp.where(lane < N_EXPERT_GROUPS, logits, NEG)
    gmax = jnp.max(gl, axis=-1, keepdims=True)
    gsel = jnp.min(jnp.where(gl == gmax, lane, big), axis=-1, keepdims=True)
    gsum = jnp.sum(jnp.exp(gl - gmax), axis=-1, keepdims=True)
    g_w = 1.0 / gsum
    e_lo = N_EXPERT_GROUPS + gsel * EXPERTS_PER_GROUP
    el = jnp.where((lane >= e_lo) & (lane < e_lo + EXPERTS_PER_GROUP), logits, NEG)
    m1 = jnp.max(el, axis=-1, keepdims=True)
    i1 = jnp.min(jnp.where(el == m1, lane, big), axis=-1, keepdims=True)
    el2 = jnp.where(lane == i1, NEG, el)
    m2 = jnp.max(el2, axis=-1, keepdims=True)
    i2 = jnp.min(jnp.where(el2 == m2, lane, big), axis=-1, keepdims=True)
    r = jnp.exp(m2 - m1)
    w1 = g_w / (1.0 + r)
    w2 = w1 * r
    e1 = i1 - N_EXPERT_GROUPS
    e2 = i2 - N_EXPERT_GROUPS
    oh1 = lane == e1
    oh2 = lane == e2
    oh = jnp.where(oh1 | oh2, 1.0, 0.0)
    before = jnp.dot(tri_ref[...], oh.astype(BF16), preferred_element_type=F32) + run_ref[0:1, :]
    rank1 = jnp.sum(jnp.where(oh1, before, 0.0), axis=-1, keepdims=True).astype(I32)
    rank2 = jnp.sum(jnp.where(oh2, before, 0.0), axis=-1, keepdims=True).astype(I32)
    run = run_ref[0:1, :] + jnp.sum(oh, axis=0, keepdims=True)
    run_ref[...] = jnp.broadcast_to(run, run_ref.shape)
    cnt_ref[...] = jnp.broadcast_to(run, cnt_ref.shape)
    eid_ref[...] = jnp.where(lane == 0, e1, jnp.where(lane == 1, e2, jnp.where(lane == 2, rank1,
                             jnp.where(lane == 3, rank2, 0))))
    gate_ref[...] = jnp.where(lane == 0, w1, jnp.where(lane == 1, w2, 0.0))


def _route(logits, tri):
    t = logits.shape[0]
    tm = tri.shape[0]
    assert t % tm == 0
    row = lambda i: (i, 0)
    const = lambda i: (0, 0)
    return pl.pallas_call(
        _route_kernel,
        out_shape=(jax.ShapeDtypeStruct((t, LANES), I32), jax.ShapeDtypeStruct((t, LANES), F32),
                   jax.ShapeDtypeStruct((SUBLANES, LANES), F32)),
        grid=(t // tm,),
        in_specs=[pl.BlockSpec((tm, LANES), row),
                  pl.BlockSpec((tm, tm), const, pipeline_mode=pl.Buffered(1))],
        out_specs=(pl.BlockSpec((tm, LANES), row), pl.BlockSpec((tm, LANES), row),
                   pl.BlockSpec((SUBLANES, LANES), const)),
        scratch_shapes=[pltpu.VMEM((SUBLANES, LANES), F32)],
        compiler_params=_cparams(("arbitrary",), 24),
        name="route",
    )(logits, tri)


def _idx_copy(src_ref, idx_smem, slot, sem):
    return pltpu.make_async_copy(src_ref.at[0], idx_smem.at[pl.ds(slot, 1)], sem)


def _dispatch_kernel(zs_ref, nused_ref, idc_ref, idn_ref, h_ref, xs_hbm, idx_smem, zbuf, hbuf, sem, isem, zsem):
    i = pl.program_id(0)
    n = pl.num_programs(0)
    slot = i % 2

    @pl.when(i == 0)
    def _():
        zbuf[...] = jnp.zeros_like(zbuf)

        def zero_block(row0):
            cp = pltpu.make_async_copy(zbuf, xs_hbm.at[pl.ds(row0, MOE_TM)], zsem)
            cp.start()
            cp.wait()

        for e in range(N_EXPERTS):
            zero_block(zs_ref[e])
        n_blocks = xs_hbm.shape[0] // MOE_TM
        for b in range(n_blocks - N_EXPERTS, n_blocks):
            @pl.when(b >= nused_ref[0])
            def _():
                zero_block(b * MOE_TM)
        cp = _idx_copy(idc_ref, idx_smem, 0, isem)
        cp.start()
        cp.wait()

    @pl.when(i + 1 < n)
    def _():
        _idx_copy(idn_ref, idx_smem, 1 - slot, isem).start()

    hbuf[slot] = h_ref[...]

    def row(r, dst_row, s):
        return pltpu.make_async_copy(hbuf.at[s, pl.ds(r, 1)], xs_hbm.at[pl.ds(dst_row, 1)], sem.at[s])

    def issue(r, carry):
        row(r, idx_smem[slot, r], slot).start(priority=0)
        row(r, idx_smem[slot, ROW_TM + r], slot).start(priority=1)
        return carry

    def drain(s):
        def body(r, carry):
            row(r, 0, s).wait()
            row(r, 0, s).wait()
            return carry
        lax.fori_loop(0, ROW_TM, body, 0, unroll=8)

    lax.fori_loop(0, ROW_TM, issue, 0, unroll=8)

    @pl.when(i + 1 < n)
    def _():
        _idx_copy(idn_ref, idx_smem, 1 - slot, isem).wait()

    @pl.when(i > 0)
    def _():
        drain(1 - slot)

    @pl.when(i + 1 == n)
    def _():
        drain(slot)


def _dispatch(h2p, dest_idx, zstart, n_used, n_slots):
    t = h2p.shape[0]
    nt = t // ROW_TM
    grid_spec = pltpu.PrefetchScalarGridSpec(
        num_scalar_prefetch=2,
        grid=(nt,),
        in_specs=[pl.BlockSpec((1, 1, TOP_K * ROW_TM), lambda i, *_: (i, 0, 0)),
                  pl.BlockSpec((1, 1, TOP_K * ROW_TM), lambda i, *_: (jnp.minimum(i + 1, nt - 1), 0, 0)),
                  pl.BlockSpec((ROW_TM,) + ROW_TILE, lambda i, *_: (i, 0, 0))],
        out_specs=pl.BlockSpec(memory_space=pl.ANY),
        scratch_shapes=[pltpu.SMEM((2, TOP_K * ROW_TM), I32), pltpu.VMEM((MOE_TM,) + ROW_TILE, BF16),
                        pltpu.VMEM((2, ROW_TM) + ROW_TILE, BF16),
                        pltpu.SemaphoreType.DMA((2,)), pltpu.SemaphoreType.DMA, pltpu.SemaphoreType.DMA],
    )
    return pl.pallas_call(
        _dispatch_kernel,
        out_shape=jax.ShapeDtypeStruct((n_slots,) + ROW_TILE, BF16),
        grid_spec=grid_spec,
        compiler_params=_cparams(("arbitrary",), 16),
        name="moe_dispatch",
    )(zstart, n_used, dest_idx, dest_idx, h2p)


def _expert_kernel(be_ref, nused_ref, x_ref, wg_ref, wu_ref, wd_ref, out_ref):
    i = pl.program_id(0)

    @pl.when(i < nused_ref[0])
    def _():
        x = _tiles_to_rows(x_ref[...])
        g = jnp.dot(x, wg_ref[...], preferred_element_type=F32)
        u = jnp.dot(x, wu_ref[...].astype(BF16), preferred_element_type=F32)
        hmid = (_silu(g) * u).astype(BF16)
        out_ref[...] = _rows_to_tiles(jnp.dot(hmid, wd_ref[...].astype(BF16), preferred_element_type=F32))

    @pl.when(i >= nused_ref[0])
    def _():
        out_ref[...] = jnp.zeros_like(out_ref)


def _experts(xs, block_expert, n_used, wg, wu, wd, layer):
    n_slots = xs.shape[0]
    n_blocks = n_slots // MOE_TM
    blk = lambda i, be, nu: (jnp.minimum(i, nu[0] - 1), 0, 0)
    wmap = lambda i, be, nu: (layer, be[i], 0, 0)
    grid_spec = pltpu.PrefetchScalarGridSpec(
        num_scalar_prefetch=2,
        grid=(n_blocks,),
        in_specs=[
            pl.BlockSpec((MOE_TM,) + ROW_TILE, blk),
            pl.BlockSpec((None, None, D_MODEL, D_FF_EXPERT), wmap),
            pl.BlockSpec((None, None, D_MODEL, D_FF_EXPERT), wmap),
            pl.BlockSpec((None, None, D_FF_EXPERT, D_MODEL), wmap),
        ],
        out_specs=pl.BlockSpec((MOE_TM,) + ROW_TILE, lambda i, be, nu: (i, 0, 0)),
    )
    return pl.pallas_call(
        _expert_kernel,
        out_shape=jax.ShapeDtypeStruct((n_slots,) + ROW_TILE, BF16),
        grid_spec=grid_spec,
        compiler_params=_cparams(("arbitrary",), 56),
        name="moe_experts",
    )(block_expert, n_used, xs, wg, wu, wd)


def _combine_kernel(id0_ref, id1_ref, idn_ref, x1_ref, gate_ref, ys_hbm, nw_ref, out_ref, idx_smem, buf_a, buf_b,
                    sem, isem, *, final_norm):
    i = pl.program_id(0)
    n = pl.num_programs(0)
    slot = i % 2

    def row(tile_slot, r, k, src_row):
        buf = buf_a if k == 0 else buf_b
        return pltpu.make_async_copy(ys_hbm.at[pl.ds(src_row, 1)], buf.at[tile_slot, pl.ds(r, 1)],
                                     sem.at[tile_slot])

    def issue_rows(tile_slot):
        def body(r, carry):
            row(tile_slot, r, 0, idx_smem[tile_slot, r]).start(priority=0)
            row(tile_slot, r, 1, idx_smem[tile_slot, ROW_TM + r]).start(priority=1)
            return carry
        lax.fori_loop(0, ROW_TM, body, 0, unroll=8)

    @pl.when(i == 0)
    def _():
        cp = _idx_copy(id0_ref, idx_smem, 0, isem)
        cp.start()
        cp.wait()
        issue_rows(0)

        @pl.when(n > 1)
        def _():
            cp1 = _idx_copy(id1_ref, idx_smem, 1, isem)
            cp1.start()
            cp1.wait()

    @pl.when(i + 1 < n)
    def _():
        issue_rows(1 - slot)

    @pl.when(i + 2 < n)
    def _():
        _idx_copy(idn_ref, idx_smem, slot, isem).start()

    def drain(r, carry):
        row(slot, r, 0, 0).wait()
        row(slot, r, 1, 0).wait()
        return carry

    lax.fori_loop(0, ROW_TM, drain, 0, unroll=8)

    @pl.when(i + 2 < n)
    def _():
        _idx_copy(idn_ref, idx_smem, slot, isem).wait()

    a = _tiles_to_rows(buf_a[slot]).astype(F32)
    b = _tiles_to_rows(buf_b[slot]).astype(F32)
    o = x1_ref[...] + (a * gate_ref[:, 0:1] + b * gate_ref[:, 1:2])
    if final_norm:
        o = _rms(o, nw_ref[...])
    out_ref[...] = o


def _combine(x1, gate, ys, dest_idx, nw, *, tile0, n_tiles, final_norm):
    last = tile0 + n_tiles - 1
    idx_block = (1, 1, TOP_K * ROW_TM)
    return pl.pallas_call(
        functools.partial(_combine_kernel, final_norm=final_norm),
        out_shape=jax.ShapeDtypeStruct((n_tiles * ROW_TM, D_MODEL), F32),
        grid=(n_tiles,),
        in_specs=[pl.BlockSpec(idx_block, lambda i: (tile0, 0, 0)),
                  pl.BlockSpec(idx_block, lambda i: (min(tile0 + 1, last), 0, 0)),
                  pl.BlockSpec(idx_block, lambda i: (jnp.minimum(tile0 + i + 2, last), 0, 0)),
                  pl.BlockSpec((ROW_TM, D_MODEL), lambda i: (tile0 + i, 0)),
                  pl.BlockSpec((ROW_TM, LANES), lambda i: (tile0 + i, 0)),
                  pl.BlockSpec(memory_space=pl.ANY),
                  pl.BlockSpec((1, D_MODEL), lambda i: (0, 0))],
        out_specs=pl.BlockSpec((ROW_TM, D_MODEL), lambda i: (i, 0)),
        scratch_shapes=[pltpu.SMEM((2, TOP_K * ROW_TM), I32), pltpu.VMEM((2, ROW_TM) + ROW_TILE, BF16),
                        pltpu.VMEM((2, ROW_TM) + ROW_TILE, BF16), pltpu.SemaphoreType.DMA((2,)),
                        pltpu.SemaphoreType.DMA],
        compiler_params=_cparams(("arbitrary",), 24),
        name="moe_combine",
    )(dest_idx, dest_idx, dest_idx, x1, gate, ys, nw)


def _dispatch_plan(eid, rank, counts):
    t = eid.shape[0]
    n_assign = t * TOP_K
    padded = (counts + MOE_TM - 1) // MOE_TM * MOE_TM
    pend = jnp.cumsum(padded)
    pstart = pend - padded
    dest = pstart[eid] + rank
    n_blocks = -(-n_assign // MOE_TM) + N_EXPERTS
    n_slots = n_blocks * MOE_TM
    block_start = jnp.arange(n_blocks, dtype=I32) * MOE_TM
    block_expert = jnp.minimum(jnp.sum((pend[None, :] <= block_start[:, None]).astype(I32), axis=1),
                               N_EXPERTS - 1).astype(I32)
    n_used = (pend[-1] // MOE_TM).astype(I32).reshape(1)
    zstart = jnp.minimum((pstart + counts) // SUBLANES * SUBLANES, n_slots - MOE_TM).astype(I32)
    dest_idx = dest.reshape(t // ROW_TM, ROW_TM, TOP_K).transpose(0, 2, 1).reshape(t // ROW_TM, 1, TOP_K * ROW_TM)
    return dest_idx.astype(I32), block_expert, n_used, zstart, n_slots


def _seq_flags(seq_lens, tile):
    first, last = [], []
    for s in seq_lens:
        assert s % tile == 0
        n = s // tile
        first += [1] + [0] * (n - 1)
        last += [0] * (n - 1) + [1]
    return jnp.asarray(first, I32), jnp.asarray(last, I32)


def _rope_tables(seq_lens):
    pos = jnp.concatenate([jnp.arange(s, dtype=F32) for s in seq_lens])
    inv_freq = ROPE_THETA ** (-jnp.arange(0, ROPE_DIM, 2, dtype=F32) / ROPE_DIM)
    ang = pos[:, None] * inv_freq[None, :]
    d = jnp.arange(LANES) % HEAD_DIM
    cos = jnp.cos(ang)[:, d % ROPE_HALF]
    sin = jnp.sin(ang)[:, d % ROPE_HALF]
    c = jnp.where(d < ROPE_DIM, cos, 1.0)
    sa = jnp.where((d >= ROPE_HALF) & (d < ROPE_DIM), sin, 0.0)
    sb = jnp.where(d < ROPE_HALF, -sin, 0.0)
    return c, sa, sb


def _head_expand(d0):
    rows = jnp.arange(LANES)[:, None]
    cols = jnp.arange(SSM_WIDTH)[None, :] // SSM_HEAD_DIM
    return (rows == cols + d0).astype(BF16)


def _pad_lanes(v, width=LANES):
    v = v.reshape(1, -1)
    return jnp.pad(v, ((0, 0), (0, width - v.shape[1])))


def _trunk(xs, seq_lens, attn_norm_w, w_in, conv_w, conv_b, attn_sink, attn_out_norm_w, ssm_a_log, ssm_dt_bias,
           ssm_d, ssm_norm_w, w_out, ffn_norm_w, w_router_group, b_router_group, w_router_expert,
           b_router_expert, w_gate, w_up, w_down, final_norm_w, out_tiles):
    depth = w_in.shape[0]
    t = sum(seq_lens)
    rope_c, rope_sa, rope_sb = _rope_tables(seq_lens)
    tile_first, tile_last = _seq_flags(seq_lens, ATTN_TQ)
    chunk_first, chunk_last = _seq_flags(seq_lens, CHUNK)
    e_fwd, e_bwd = _head_expand(0), _head_expand(SSM_HEADS)
    shift = _conv_shift_matrix()
    tri = (jnp.arange(ROUTE_TM)[:, None] > jnp.arange(ROUTE_TM)[None, :]).astype(BF16)
    z_end = ATTN_WIDTH + 2 * KV_WIDTH + SSM_WIDTH
    w_out_b, w_gate_b = w_out.astype(BF16), w_gate.astype(BF16)
    outs = None
    for l in range(depth):
        w = w_in[l]
        w_main = jnp.concatenate([w[:, :ATTN_WIDTH + 2 * KV_WIDTH], w[:, z_end:z_end + CONV_CH],
                                  w[:, ATTN_WIDTH + 2 * KV_WIDTH:z_end]], axis=1).astype(BF16)
        w_dt = jnp.pad(w[:, z_end + CONV_CH:], ((0, 0), (0, LANES - 2 * SSM_HEADS))).astype(BF16)
        proj, dt_raw = _proj(xs, attn_norm_w[l].reshape(1, -1), w_main, w_dt, rope_c, rope_sa, rope_sb)
        attn = _attention(proj, tile_first, tile_last, attn_sink[l] * LOG2E, attn_out_norm_w[l].reshape(1, -1))
        cw = jnp.pad(conv_w[l], ((0, SUBLANES - CONV_WIDTH), (0, 0)))
        cb = conv_b[l].reshape(1, -1)
        dtb = _pad_lanes(ssm_dt_bias[l])
        alog = _pad_lanes(ssm_a_log[l])
        dexp = jnp.repeat(ssm_d[l], SSM_HEAD_DIM).reshape(1, -1)
        y_fwd, u_fwd = _ssd(proj, dt_raw, chunk_first, chunk_last, dtb, alog, e_fwd, dexp, backward=False,
                            conv=(shift, cw, cb))
        ssd = _ssd(proj, dt_raw, chunk_first, chunk_last, dtb, alog, e_bwd, ssm_norm_w[l].reshape(1, -1),
                   backward=True, u_fwd=u_fwd, y_fwd=y_fwd)
        wr = jnp.pad(jnp.concatenate([w_router_group[l], w_router_expert[l]], axis=1),
                     ((0, 0), (0, LANES - N_EXPERT_GROUPS - N_EXPERTS)))
        wr_hi, wr_lo = _split2(wr)
        br = _pad_lanes(jnp.concatenate([b_router_group[l], b_router_expert[l]]))
        x1, h2p, logits = _outproj(attn, ssd, xs, w_out_b, l, ffn_norm_w[l].reshape(1, -1),
                                   jnp.concatenate([wr_hi, wr_lo], axis=1), wr_hi, br)
        eid, gate, cnt = _route(logits, tri)
        counts = cnt[0, :N_EXPERTS].astype(I32)
        dest_idx, block_expert, n_used, zstart, n_slots = _dispatch_plan(eid[:, :TOP_K], eid[:, TOP_K:2 * TOP_K],
                                                                        counts)
        xslots = _dispatch(h2p, dest_idx, zstart, n_used, n_slots)
        ys = _experts(xslots, block_expert, n_used, w_gate_b, w_up, w_down, l)
        nw = final_norm_w.reshape(1, -1)
        if l == depth - 1:
            outs, tile0 = [], 0
            for nt in out_tiles:
                outs.append(_combine(x1, gate, ys, dest_idx, nw, tile0=tile0, n_tiles=nt, final_norm=True))
                tile0 += nt
        else:
            xs = [_combine(x1, gate, ys, dest_idx, nw, tile0=0, n_tiles=t // ROW_TM, final_norm=False)]
    return outs


def kernel(x_prompt, x_sample, attn_norm_w, w_in, conv_w, conv_b, attn_sink, attn_out_norm_w, ssm_a_log,
           ssm_dt_bias, ssm_d, ssm_norm_w, w_out, ffn_norm_w, w_router_group, b_router_group, w_router_expert,
           b_router_expert, w_gate, w_up, w_down, final_norm_w):
    bp, sp, d = x_prompt.shape
    bs, ss, _ = x_sample.shape
    seq_lens = [sp] * bp + [ss] * bs
    xs = [x_prompt.reshape(bp * sp, d), x_sample.reshape(bs * ss, d)]
    y_prompt, y_sample = _trunk(xs, seq_lens, attn_norm_w, w_in, conv_w, conv_b, attn_sink, attn_out_norm_w,
                                ssm_a_log, ssm_dt_bias, ssm_d, ssm_norm_w, w_out, ffn_norm_w, w_router_group,
                                b_router_group, w_router_expert, b_router_expert, w_gate, w_up, w_down,
                                final_norm_w, out_tiles=[bp * sp // ROW_TM, bs * ss // ROW_TM])
    return (y_prompt.reshape(bp, sp, d), y_sample.reshape(bs, ss, d))
```
